```python
import jax, jax.numpy as jnp
from jax import lax
import numpy as np

D_MODEL = 1024
BATCH = 8
SEQ = 4096
DEPTH = 2

EPS = 1e-6
ROPE_THETA = 10000.0
Q_BLOCK = 128

LRU_WIDTH = 1024
LRU_BLOCKS = 8
LRU_BLOCK_W = LRU_WIDTH // LRU_BLOCKS
CONV_WIDTH = 4
LRU_C = 8.0

MLA_HEADS = 8
MLA_NOPE = 64
MLA_ROPE = 32
MLA_QK = MLA_NOPE + MLA_ROPE
MLA_V = 64
Q_LORA = 256
KV_LORA = 128
MLA_WIDTH = MLA_HEADS * MLA_V

DIL_GROUPS = ((128, 1), (512, 4), (2048, 16))
DIL_HEADS = 8
DIL_HD = 64
DIL_QKV = len(DIL_GROUPS) * DIL_HEADS * DIL_HD
DIL_WIDTH = DIL_HEADS * DIL_HD

N_BRANCH = 3
SPLITS = (LRU_WIDTH, LRU_WIDTH, Q_LORA, KV_LORA, MLA_ROPE, MLA_WIDTH,
          DIL_QKV, DIL_QKV, DIL_QKV, DIL_WIDTH, N_BRANCH * D_MODEL)
IN_WIDTH = (2 * LRU_WIDTH + Q_LORA + KV_LORA + MLA_ROPE + MLA_WIDTH
            + 3 * DIL_QKV + DIL_WIDTH + N_BRANCH * D_MODEL)

kernel_name = 'hybrid_rglru_mla_dilated_swa'


def rms_norm(x, g):
    xf = x.astype(jnp.float32)
    y = xf * lax.rsqrt(jnp.mean(xf * xf, axis=-1, keepdims=True) + EPS)
    return (y * g.astype(jnp.float32)).astype(x.dtype)


def rotary(x, pos):
    d = x.shape[-1]
    inv = ROPE_THETA ** (-jnp.arange(0, d, 2, dtype=jnp.float32) / d)
    ang = pos.astype(jnp.float32)[..., None] * inv
    cos = jnp.cos(ang)[:, :, None, :]
    sin = jnp.sin(ang)[:, :, None, :]
    xf = x.astype(jnp.float32)
    x1, x2 = xf[..., : d // 2], xf[..., d // 2:]
    return jnp.concatenate([x1 * cos - x2 * sin, x2 * cos + x1 * sin], axis=-1).astype(x.dtype)


def rg_lru_branch(xb, conv_w, conv_b, w_gx, b_gx, w_ga, b_ga, lam):
    B, S, _ = xb.shape
    xp = jnp.pad(xb, ((0, 0), (CONV_WIDTH - 1, 0), (0, 0)))
    xc = conv_b
    for k in range(CONV_WIDTH):
        xc = xc + xp[:, k:k + S] * conv_w[k]
    xblk = xc.reshape(B, S, LRU_BLOCKS, LRU_BLOCK_W)
    gx = jax.nn.sigmoid(jnp.einsum('bsnc,ncd->bsnd', xblk, w_gx) + b_gx).reshape(B, S, LRU_WIDTH)
    ga = jax.nn.sigmoid(jnp.einsum('bsnc,ncd->bsnd', xblk, w_ga) + b_ga).reshape(B, S, LRU_WIDTH)
    log_a = -LRU_C * ga.astype(jnp.float32) * jax.nn.softplus(-lam.astype(jnp.float32))
    a = jnp.exp(log_a)
    mult = jnp.sqrt(-jnp.expm1(2.0 * log_a))
    b = mult * (gx * xc).astype(jnp.float32)

    def combine(left, right):
        a1, b1 = left
        a2, b2 = right
        return a1 * a2, a2 * b1 + b2

    _, h = lax.associative_scan(combine, (a, b), axis=1)
    return h.astype(xb.dtype)


def causal_attention(q, k, v, scale):
    B, S, H, dk = q.shape
    nb = S // Q_BLOCK
    qb = q.reshape(B, nb, Q_BLOCK, H, dk).transpose(1, 0, 3, 2, 4)
    kt = k.transpose(0, 2, 1, 3)
    vt = v.transpose(0, 2, 1, 3)
    kpos = jnp.arange(S)

    def one_block(args):
        qi, i = args
        s = jnp.einsum('bhqd,bhkd->bhqk', qi, kt).astype(jnp.float32) * scale
        qpos = i * Q_BLOCK + jnp.arange(Q_BLOCK)
        s = jnp.where(kpos[None, :] <= qpos[:, None], s, -jnp.inf)
        p = jax.nn.softmax(s, axis=-1)
        return jnp.einsum('bhqk,bhkd->bhqd', p.astype(vt.dtype), vt)

    o = lax.map(one_block, (qb, jnp.arange(nb)))
    return o.transpose(1, 0, 3, 2, 4).reshape(B, S, H, -1)


def mla_branch(cq, ckv, kr, pos, g_cq, g_ckv, w_uq, w_ukv, g_qn, g_kn):
    B, S, _ = cq.shape
    q = (rms_norm(cq, g_cq) @ w_uq).reshape(B, S, MLA_HEADS, MLA_QK)
    kv = (rms_norm(ckv, g_ckv) @ w_ukv).reshape(B, S, MLA_HEADS, MLA_NOPE + MLA_V)
    k_nope, v = kv[..., :MLA_NOPE], kv[..., MLA_NOPE:]
    k_rope = jnp.broadcast_to(kr[:, :, None, :], (B, S, MLA_HEADS, MLA_ROPE))
    k = jnp.concatenate([k_nope, k_rope], axis=-1)
    q = rms_norm(q, g_qn)
    k = rms_norm(k, g_kn)
    q = jnp.concatenate([q[..., :MLA_NOPE], rotary(q[..., MLA_NOPE:], pos)], axis=-1)
    k = jnp.concatenate([k[..., :MLA_NOPE], rotary(k[..., MLA_NOPE:], pos)], axis=-1)
    o = causal_attention(q, k, v, MLA_QK ** -0.5)
    return o.reshape(B, S, MLA_WIDTH)


def dilated_group(q, k, v, window, dilation):
    B, S, H, d = q.shape
    nk = window // dilation
    span = dilation * nk
    s_pad = -(-S // span) * span
    M = s_pad // dilation
    nb = M // nk

    def to_strided(t):
        t = jnp.pad(t, ((0, 0), (0, s_pad - S), (0, 0), (0, 0))).reshape(B, M, dilation, H, d)
        return t.transpose(0, 2, 3, 1, 4).reshape(B, dilation, H, nb, nk, d)

    qb, kb, vb = to_strided(q), to_strided(k), to_strided(v)

    def prev(t):
        return jnp.pad(t, ((0, 0), (0, 0), (0, 0), (1, 0), (0, 0), (0, 0)))[:, :, :, :-1]

    kw = jnp.concatenate([prev(kb), kb], axis=4)
    vw = jnp.concatenate([prev(vb), vb], axis=4)
    s = jnp.einsum('brhnqd,brhnkd->brhnqk', qb, kw).astype(jnp.float32) * (DIL_HD ** -0.5)
    qi = jnp.arange(nk)[:, None]
    ki = jnp.arange(2 * nk)[None, :]
    band = (ki >= qi) & (ki <= qi + nk)
    not_first = jnp.arange(nb)[:, None, None] > 0
    mask = band[None] & (not_first | (ki >= nk)[None])
    s = jnp.where(mask, s, -jnp.inf)
    m = jnp.max(s, axis=-1, keepdims=True)
    e = jnp.exp(s - m)
    den = jnp.sum(e, axis=-1, keepdims=True)
    o = jnp.einsum('brhnqk,brhnkd->brhnqd', (e / den).astype(vw.dtype), vw)
    lse = (m + jnp.log(den))[..., 0]
    o = o.reshape(B, dilation, H, M, d).transpose(0, 3, 1, 2, 4).reshape(B, s_pad, H, d)[:, :S]
    lse = lse.reshape(B, dilation, H, M).transpose(0, 3, 1, 2).reshape(B, s_pad, H)[:, :S]
    return o, lse


def dilated_branch(q, k, v, pos, g_qn, g_kn):
    B, S, _ = q.shape
    nh = len(DIL_GROUPS) * DIL_HEADS
    q = rotary(rms_norm(q.reshape(B, S, nh, DIL_HD), g_qn), pos)
    k = rotary(rms_norm(k.reshape(B, S, nh, DIL_HD), g_kn), pos)
    v = v.reshape(B, S, nh, DIL_HD)
    outs, lses = [], []
    for gi, (window, dilation) in enumerate(DIL_GROUPS):
        sl = slice(gi * DIL_HEADS, (gi + 1) * DIL_HEADS)
        o, l = dilated_group(q[:, :, sl], k[:, :, sl], v[:, :, sl], window, dilation)
        outs.append(o)
        lses.append(l)
    wts = jax.nn.softmax(jnp.stack(lses, axis=0), axis=0)
    o = jnp.sum(wts[..., None].astype(v.dtype) * jnp.stack(outs, axis=0), axis=0)
    return o.reshape(B, S, DIL_WIDTH)


def hybrid_layer(x, pos, norm_g, w_in, conv_w, conv_b, w_gx, b_gx, w_ga, b_ga, lam, w_lru_o,
                 g_cq, g_ckv, w_uq, w_ukv, g_mqn, g_mkn, w_mla_o, g_dqn, g_dkn, w_dil_o,
                 b_merge, w_out):
    B, S, _ = x.shape
    h = rms_norm(x, norm_g)
    z = h @ w_in
    idx = np.cumsum(SPLITS)[:-1].tolist()
    (lru_x, lru_g, cq, ckv, kr, mla_g, dq, dk, dv, dil_g, merge) = jnp.split(z, idx, axis=-1)
    y_lru = rg_lru_branch(lru_x, conv_w, conv_b, w_gx, b_gx, w_ga, b_ga, lam) * jax.nn.silu(lru_g)
    y_mla = mla_branch(cq, ckv, kr, pos, g_cq, g_ckv, w_uq, w_ukv, g_mqn, g_mkn) * jax.nn.silu(mla_g)
    y_dil = dilated_branch(dq, dk, dv, pos, g_dqn, g_dkn) * jax.nn.silu(dil_g)
    gates = jax.nn.sigmoid(merge + b_merge).reshape(B, S, N_BRANCH, D_MODEL)
    merged = (gates[:, :, 0] * (y_lru @ w_lru_o)
              + gates[:, :, 1] * (y_mla @ w_mla_o)
              + gates[:, :, 2] * (y_dil @ w_dil_o))
    return x + merged @ w_out


def setup_inputs(seed: int = 0) -> dict:
    key = jax.random.key(seed)
    ks = jax.random.split(key, 24)

    def nrm(k, shape, scale):
        return jax.random.normal(k, shape, jnp.float32) * scale

    def gain(k, shape):
        return 1.0 + 0.05 * jax.random.normal(k, shape, jnp.float32)

    x = nrm(ks[0], (BATCH, SEQ, D_MODEL), 1.0)
    offsets = jax.random.randint(ks[1], (BATCH, 1), 0, 1024, dtype=jnp.int32)
    positions = offsets + jnp.arange(SEQ, dtype=jnp.int32)[None, :]
    a0 = jax.random.uniform(ks[8], (DEPTH, LRU_WIDTH), jnp.float32, 0.9, 0.999)
    return {
        'x': x,
        'positions': positions,
        'norm_g': gain(ks[2], (DEPTH, D_MODEL)),
        'w_in': nrm(ks[3], (DEPTH, D_MODEL, IN_WIDTH), D_MODEL ** -0.5),
        'conv_w': nrm(ks[4], (DEPTH, CONV_WIDTH, LRU_WIDTH), CONV_WIDTH ** -0.5),
        'conv_b': nrm(ks[5], (DEPTH, LRU_WIDTH), 0.02),
        'w_gate_x': nrm(ks[6], (DEPTH, LRU_BLOCKS, LRU_BLOCK_W, LRU_BLOCK_W), LRU_BLOCK_W ** -0.5),
        'b_gate_x': nrm(ks[7], (DEPTH, LRU_BLOCKS, LRU_BLOCK_W), 0.1),
        'w_gate_a': nrm(ks[9], (DEPTH, LRU_BLOCKS, LRU_BLOCK_W, LRU_BLOCK_W), LRU_BLOCK_W ** -0.5),
        'b_gate_a': nrm(ks[10], (DEPTH, LRU_BLOCKS, LRU_BLOCK_W), 0.1),
        'lru_lambda': jnp.log(a0) - jnp.log1p(-a0),
        'w_lru_o': nrm(ks[11], (DEPTH, LRU_WIDTH, D_MODEL), LRU_WIDTH ** -0.5),
        'cq_norm_g': gain(ks[12], (DEPTH, Q_LORA)),
        'ckv_norm_g': gain(ks[13], (DEPTH, KV_LORA)),
        'w_uq': nrm(ks[14], (DEPTH, Q_LORA, MLA_HEADS * MLA_QK), Q_LORA ** -0.5),
        'w_ukv': nrm(ks[15], (DEPTH, KV_LORA, MLA_HEADS * (MLA_NOPE + MLA_V)), KV_LORA ** -0.5),
        'mla_q_norm_g': gain(ks[16], (DEPTH, MLA_QK)),
        'mla_k_norm_g': gain(ks[17], (DEPTH, MLA_QK)),
        'w_mla_o': nrm(ks[18], (DEPTH, MLA_WIDTH, D_MODEL), MLA_WIDTH ** -0.5),
        'dil_q_norm_g': gain(ks[19], (DEPTH, DIL_HD)),
        'dil_k_norm_g': gain(ks[20], (DEPTH, DIL_HD)),
        'w_dil_o': nrm(ks[21], (DEPTH, DIL_WIDTH, D_MODEL), DIL_WIDTH ** -0.5),
        'b_merge': nrm(ks[22], (DEPTH, N_BRANCH * D_MODEL), 0.1),
        'w_out': nrm(ks[23], (DEPTH, D_MODEL, D_MODEL), D_MODEL ** -0.5),
    }


def reference(x, positions, norm_g, w_in, conv_w, conv_b, w_gate_x, b_gate_x, w_gate_a, b_gate_a,
              lru_lambda, w_lru_o, cq_norm_g, ckv_norm_g, w_uq, w_ukv, mla_q_norm_g, mla_k_norm_g,
              w_mla_o, dil_q_norm_g, dil_k_norm_g, w_dil_o, b_merge, w_out):
    for l in range(DEPTH):
        x = hybrid_layer(x, positions, norm_g[l], w_in[l], conv_w[l], conv_b[l],
                         w_gate_x[l], b_gate_x[l], w_gate_a[l], b_gate_a[l], lru_lambda[l],
                         w_lru_o[l], cq_norm_g[l], ckv_norm_g[l], w_uq[l], w_ukv[l],
                         mla_q_norm_g[l], mla_k_norm_g[l], w_mla_o[l], dil_q_norm_g[l],
                         dil_k_norm_g[l], w_dil_o[l], b_merge[l], w_out[l])
    return x
```

```python
import functools

import numpy as np
import jax
import jax.numpy as jnp
from jax import lax
from jax.experimental import pallas as pl
from jax.experimental.pallas import tpu as pltpu

F32 = jnp.float32
BF16 = jnp.bfloat16

D_MODEL = 1024
DEPTH = 2
EPS = 1e-6
ROPE_THETA = 10000.0

LRU_WIDTH = 1024
LRU_BLOCKS = 8
LRU_BLOCK_W = LRU_WIDTH // LRU_BLOCKS
CONV_WIDTH = 4
LRU_C = 8.0

MLA_HEADS = 8
MLA_NOPE = 64
MLA_ROPE = 32
MLA_QK = MLA_NOPE + MLA_ROPE
MLA_V = 64
Q_LORA = 256
KV_LORA = 128
MLA_WIDTH = MLA_HEADS * MLA_V

DIL_GROUPS = ((128, 1), (512, 4), (2048, 16))
DIL_HEADS = 8
DIL_HD = 64
DIL_GW = DIL_HEADS * DIL_HD
DIL_QKV = len(DIL_GROUPS) * DIL_GW
DIL_WIDTH = DIL_GW
DIL_NK = 128

N_BRANCH = 3
OFF_LRU_X = 0
OFF_LRU_G = OFF_LRU_X + LRU_WIDTH
OFF_CQ = OFF_LRU_G + LRU_WIDTH
OFF_CKV = OFF_CQ + Q_LORA
OFF_KR = OFF_CKV + KV_LORA
OFF_MLA_G = OFF_KR + MLA_ROPE
OFF_DQ = OFF_MLA_G + MLA_WIDTH
OFF_DK = OFF_DQ + DIL_QKV
OFF_DV = OFF_DK + DIL_QKV
OFF_DIL_G = OFF_DV + DIL_QKV
OFF_MERGE = OFF_DIL_G + DIL_WIDTH

LANES = 128
SUBLANES = 8
VMEM_LIMIT = 48 * 1024 * 1024

NEG_INF = float("-inf")


def _dil_pair_perm():
    half = DIL_HD // 2
    cols = []
    for p in range(DIL_HEADS // 2):
        for quarter in range(4):
            head = 2 * p + (quarter % 2)
            cols.append(head * DIL_HD + half * (quarter // 2) + np.arange(half))
    return np.concatenate(cols)


def _mla_lane_feat():
    feat = -np.ones((LANES,), np.int64)
    feat[0:48] = np.arange(48)
    feat[48:64] = MLA_NOPE + np.arange(16)
    feat[64:80] = 48 + np.arange(16)
    feat[112:128] = MLA_NOPE + 16 + np.arange(16)
    return feat


_DIL_PERM = _dil_pair_perm()
_DIL_FEAT = _DIL_PERM[:LANES] % DIL_HD
_MLA_FEAT = _mla_lane_feat()


def _block_ones(width, head_of_lane):
    m = (head_of_lane[:, None] == head_of_lane[None, :]).astype(np.float32)
    assert m.shape == (width, width)
    return m


_DIL_HEAD_OF_LANE = (np.arange(2 * LANES) // LANES) * 2 + (np.arange(2 * LANES) % LANES // 32) % 2
_DIL_ONES = _block_ones(2 * LANES, _DIL_HEAD_OF_LANE)
_MLA_ONES = _block_ones(2 * LANES, np.arange(2 * LANES) // LANES)


def _rms(xf, g_row):
    y = xf * lax.rsqrt(jnp.mean(xf * xf, axis=-1, keepdims=True) + EPS)
    return y * g_row


def _split_dot(a_f32, m_bf16):
    hi = a_f32.astype(BF16)
    lo = (a_f32 - hi.astype(F32)).astype(BF16)
    return (jnp.dot(hi, m_bf16, preferred_element_type=F32)
            + jnp.dot(lo, m_bf16, preferred_element_type=F32))


def _head_rms_rope(z, ones_ref, head_dim, g_row, cos, sin):
    n = z.shape[-1]
    parts = []
    for c in range(n // (2 * LANES)):
        zc = z[:, c * 2 * LANES:(c + 1) * 2 * LANES]
        ss = _split_dot(zc * zc, ones_ref[...])
        parts.append(zc * lax.rsqrt(ss * (1.0 / head_dim) + EPS))
    y = jnp.concatenate(parts, axis=-1) * g_row
    sw = jnp.concatenate(
        [pltpu.roll(y[:, c * LANES:(c + 1) * LANES], LANES // 2, 1) for c in range(n // LANES)],
        axis=-1)
    return y * cos + sw * sin


def _rope_kernel(pos_ref, inv_ref, sgn_ref, cos_ref, sin_ref):
    ang = pos_ref[0].astype(F32) * inv_ref[...]
    cos_ref[0] = jnp.cos(ang)
    sin_ref[0] = jnp.sin(ang) * sgn_ref[...]


def _rope_tables(positions, tm):
    B, S = positions.shape
    inv_d = ROPE_THETA ** (-jnp.arange(0, DIL_HD, 2, dtype=F32) / DIL_HD)
    inv_m = ROPE_THETA ** (-jnp.arange(0, MLA_ROPE, 2, dtype=F32) / MLA_ROPE)
    lane = np.arange(LANES)
    inv_d_l = inv_d[lane % (DIL_HD // 2)]
    sgn_d_l = np.where(lane < LANES // 2, -1.0, 1.0)
    rope_feat = _MLA_FEAT - MLA_NOPE
    is_rope = rope_feat >= 0
    inv_m_l = jnp.where(is_rope, inv_m[np.where(is_rope, rope_feat % (MLA_ROPE // 2), 0)], 0.0)
    sgn_m_l = np.where(is_rope, np.where(rope_feat < MLA_ROPE // 2, -1.0, 1.0), 0.0)
    inv = jnp.concatenate([inv_d_l, inv_m_l])[None, :].astype(F32)
    sgn = jnp.asarray(np.concatenate([sgn_d_l, sgn_m_l])[None, :], F32)
    out = jax.ShapeDtypeStruct((B, S, 2 * LANES), F32)
    return pl.pallas_call(
        _rope_kernel,
        grid=(B, S // tm),
        in_specs=[pl.BlockSpec((1, tm, 1), lambda b, i: (b, i, 0)),
                  pl.BlockSpec((1, 2 * LANES), lambda b, i: (0, 0)),
                  pl.BlockSpec((1, 2 * LANES), lambda b, i: (0, 0))],
        out_specs=[pl.BlockSpec((1, tm, 2 * LANES), lambda b, i: (b, i, 0)),
                   pl.BlockSpec((1, tm, 2 * LANES), lambda b, i: (b, i, 0))],
        out_shape=[out, out],
        compiler_params=pltpu.CompilerParams(
            dimension_semantics=("parallel", "parallel"), vmem_limit_bytes=VMEM_LIMIT),
        name="rope_tables",
    )(positions.reshape(B, S, 1), inv, sgn)


def _lru_kernel(x_ref, ng_ref, w_ref, cw_ref, cb_ref, wgx_ref, bgx_ref, wga_ref, bga_ref,
                lam_ref, y_ref, xbuf, a_scr, b_scr, h_scr, hcar, *, ts):
    t = pl.program_id(1)

    @pl.when(t == 0)
    def _():
        xbuf[0:SUBLANES, :] = jnp.zeros((SUBLANES, LRU_WIDTH), F32)
        hcar[...] = jnp.zeros((1, LRU_WIDTH), F32)

    h = _rms(x_ref[0], ng_ref[...]).astype(BF16)
    z = jnp.dot(h, w_ref[...], preferred_element_type=F32)
    lx = z[:, :LRU_WIDTH]
    gate = z[:, LRU_WIDTH:]
    xbuf[SUBLANES:SUBLANES + ts, :] = lx
    xc = cb_ref[...]
    for k in range(CONV_WIDTH):
        lo = SUBLANES - (CONV_WIDTH - 1) + k
        xc = xc + xbuf[lo:lo + ts, :] * cw_ref[k:k + 1, :]
    xbuf[0:SUBLANES, :] = xbuf[ts:ts + SUBLANES, :]

    lam = lam_ref[...]
    neg_sp = -(jnp.maximum(-lam, 0.0) + jnp.log1p(jnp.exp(-jnp.abs(lam))))
    for n in range(LRU_BLOCKS):
        sl = slice(n * LRU_BLOCK_W, (n + 1) * LRU_BLOCK_W)
        xb = xc[:, sl]
        xb16 = xb.astype(BF16)
        gx = jax.nn.sigmoid(jnp.dot(xb16, wgx_ref[n], preferred_element_type=F32) + bgx_ref[n])
        ga = jax.nn.sigmoid(jnp.dot(xb16, wga_ref[n], preferred_element_type=F32) + bga_ref[n])
        log_a = LRU_C * ga * neg_sp[:, sl]
        a = jnp.exp(log_a)
        mult = jnp.sqrt(-jnp.tanh(log_a) * (1.0 + a * a))
        a_scr[:, sl] = a
        b_scr[:, sl] = mult * (gx * xb)

    row = lax.broadcasted_iota(jnp.int32, (SUBLANES, LRU_WIDTH), 0)

    def group(g, carry):
        r = pl.multiple_of(g * SUBLANES, SUBLANES)
        a8 = a_scr[pl.ds(r, SUBLANES), :]
        b8 = b_scr[pl.ds(r, SUBLANES), :]
        for s in (1, 2, 4):
            keep = row >= s
            a_prev = pltpu.roll(a8, s, 0)
            b_prev = pltpu.roll(b8, s, 0)
            b8 = jnp.where(keep, a8 * b_prev + b8, b8)
            a8 = jnp.where(keep, a8 * a_prev, a8)
        h8 = a8 * carry + b8
        h_scr[pl.ds(r, SUBLANES), :] = h8
        return h8[SUBLANES - 1:SUBLANES, :]

    hcar[...] = lax.fori_loop(0, ts // SUBLANES, group, hcar[...], unroll=2)
    y_ref[0] = (h_scr[...] * (gate * jax.nn.sigmoid(gate))).astype(BF16)


def _lru_call(x, ng, w, cw, cb, wgx, bgx, wga, bga, lam, ts):
    B, S, _ = x.shape
    const2 = lambda b, t: (0, 0)
    const3 = lambda b, t: (0, 0, 0)
    return pl.pallas_call(
        functools.partial(_lru_kernel, ts=ts),
        grid=(B, S // ts),
        in_specs=[pl.BlockSpec((1, ts, D_MODEL), lambda b, t: (b, t, 0)),
                  pl.BlockSpec((1, D_MODEL), const2),
                  pl.BlockSpec((D_MODEL, 2 * LRU_WIDTH), const2),
                  pl.BlockSpec((CONV_WIDTH, LRU_WIDTH), const2),
                  pl.BlockSpec((1, LRU_WIDTH), const2),
                  pl.BlockSpec((LRU_BLOCKS, LRU_BLOCK_W, LRU_BLOCK_W), const3),
                  pl.BlockSpec((LRU_BLOCKS, 1, LRU_BLOCK_W), const3),
                  pl.BlockSpec((LRU_BLOCKS, LRU_BLOCK_W, LRU_BLOCK_W), const3),
                  pl.BlockSpec((LRU_BLOCKS, 1, LRU_BLOCK_W), const3),
                  pl.BlockSpec((1, LRU_WIDTH), const2)],
        out_specs=pl.BlockSpec((1, ts, LRU_WIDTH), lambda b, t: (b, t, 0)),
        out_shape=jax.ShapeDtypeStruct((B, S, LRU_WIDTH), BF16),
        scratch_shapes=[pltpu.VMEM((ts + SUBLANES, LRU_WIDTH), F32),
                        pltpu.VMEM((ts, LRU_WIDTH), F32),
                        pltpu.VMEM((ts, LRU_WIDTH), F32),
                        pltpu.VMEM((ts, LRU_WIDTH), F32),
                        pltpu.VMEM((1, LRU_WIDTH), F32)],
        compiler_params=pltpu.CompilerParams(
            dimension_semantics=("parallel", "arbitrary"), vmem_limit_bytes=VMEM_LIMIT),
        name="lru_branch",
    )(x, ng, w, cw, cb, wgx, bgx, wga, bga, lam)


def _mla_prep_kernel(x_ref, ng_ref, wb_ref, gcq_ref, gckv_ref, wuq_ref, wuk_ref, wuv_ref,
                     gq_ref, gk_ref, ones_ref, cos_ref, sin_ref, q_ref, k_ref, v_ref):
    h = _rms(x_ref[0], ng_ref[...]).astype(BF16)
    z = jnp.dot(h, wb_ref[...], preferred_element_type=F32)
    cq = _rms(z[:, :Q_LORA], gcq_ref[...]).astype(BF16)
    ckv = _rms(z[:, Q_LORA:Q_LORA + KV_LORA], gckv_ref[...]).astype(BF16)
    kr = z[:, Q_LORA + KV_LORA:]
    cos = jnp.concatenate([cos_ref[0]] * MLA_HEADS, axis=-1)
    sin = jnp.concatenate([sin_ref[0]] * MLA_HEADS, axis=-1)
    q = jnp.dot(cq, wuq_ref[...], preferred_element_type=F32)
    q = _head_rms_rope(q, ones_ref, MLA_QK, gq_ref[...], cos, sin) * (MLA_QK ** -0.5)
    k = jnp.dot(ckv, wuk_ref[...], preferred_element_type=F32)
    k = k + jnp.concatenate([kr] * MLA_HEADS, axis=-1)
    k = _head_rms_rope(k, ones_ref, MLA_QK, gk_ref[...], cos, sin)
    for hh in range(MLA_HEADS):
        q_ref[0, hh] = q[:, hh * LANES:(hh + 1) * LANES].astype(BF16)
        k_ref[0, hh] = k[:, hh * LANES:(hh + 1) * LANES].astype(BF16)
    v_ref[0] = jnp.dot(ckv, wuv_ref[...], preferred_element_type=F32).astype(BF16)


def _mla_prep_call(x, ng, wb, gcq, gckv, wuq, wuk, wuv, gq, gk, ones, cos_t, sin_t, tm):
    B, S, _ = x.shape
    c2 = lambda b, i: (0, 0)
    hw = MLA_HEADS * LANES
    return pl.pallas_call(
        _mla_prep_kernel,
        grid=(B, S // tm),
        in_specs=[pl.BlockSpec((1, tm, D_MODEL), lambda b, i: (b, i, 0)),
                  pl.BlockSpec((1, D_MODEL), c2),
                  pl.BlockSpec(wb.shape, c2),
                  pl.BlockSpec((1, Q_LORA), c2),
                  pl.BlockSpec((1, KV_LORA), c2),
                  pl.BlockSpec((Q_LORA, hw), c2),
                  pl.BlockSpec((KV_LORA, hw), c2),
                  pl.BlockSpec((KV_LORA, MLA_WIDTH), c2),
                  pl.BlockSpec((1, hw), c2),
                  pl.BlockSpec((1, hw), c2),
                  pl.BlockSpec((2 * LANES, 2 * LANES), c2),
                  pl.BlockSpec((1, tm, LANES), lambda b, i: (b, i, 1)),
                  pl.BlockSpec((1, tm, LANES), lambda b, i: (b, i, 1))],
        out_specs=[pl.BlockSpec((1, MLA_HEADS, tm, LANES), lambda b, i: (b, 0, i, 0)),
                   pl.BlockSpec((1, MLA_HEADS, tm, LANES), lambda b, i: (b, 0, i, 0)),
                   pl.BlockSpec((1, tm, MLA_WIDTH), lambda b, i: (b, i, 0))],
        out_shape=[jax.ShapeDtypeStruct((B, MLA_HEADS, S, LANES), BF16),
                   jax.ShapeDtypeStruct((B, MLA_HEADS, S, LANES), BF16),
                   jax.ShapeDtypeStruct((B, S, MLA_WIDTH), BF16)],
        compiler_params=pltpu.CompilerParams(
            dimension_semantics=("parallel", "parallel"), vmem_limit_bytes=VMEM_LIMIT),
        name="mla_prep",
    )(x, ng, wb, gcq, gckv, wuq, wuk, wuv, gq, gk, ones, cos_t, sin_t)


def _mla_attn_kernel(q_ref, k_ref, v_ref, o_ref, *, tq):
    qi = pl.program_id(2)
    row = lax.broadcasted_iota(jnp.int32, (tq, tq), 0)
    col = lax.broadcasted_iota(jnp.int32, (tq, tq), 1)
    causal = col <= row
    outs = []
    for hh in range(2):
        q = q_ref[0, hh]

        def step(j, carry, masked):
            m, l, acc = carry
            start = pl.multiple_of(j * tq, tq)
            k = k_ref[0, hh, pl.ds(start, tq), :]
            v = v_ref[0, pl.ds(start, tq), :]
            s = lax.dot_general(q, k, (((1,), (1,)), ((), ())), preferred_element_type=F32)
            if masked:
                s = jnp.where(causal, s, NEG_INF)
            m_new = jnp.maximum(m, jnp.max(s, axis=-1, keepdims=True))
            alpha = jnp.exp(m - m_new)
            p = jnp.exp(s - m_new)
            l = alpha * l + jnp.sum(p, axis=-1, keepdims=True)
            acc = alpha * acc + jnp.dot(p.astype(BF16), v, preferred_element_type=F32)
            return m_new, l, acc

        init = (jnp.full((tq, 1), NEG_INF, F32), jnp.zeros((tq, 1), F32),
                jnp.zeros((tq, LANES), F32))
        carry = lax.fori_loop(0, qi, functools.partial(step, masked=False), init)
        _, l, acc = step(qi, carry, True)
        outs.append(acc / l)
    lane = lax.broadcasted_iota(jnp.int32, (tq, LANES), 1)
    o_ref[0] = jnp.where(lane < MLA_V, outs[0], outs[1]).astype(BF16)


def _mla_attn_call(q, k, v, tq):
    B, H, S, _ = q.shape
    return pl.pallas_call(
        functools.partial(_mla_attn_kernel, tq=tq),
        grid=(B, H // 2, S // tq),
        in_specs=[pl.BlockSpec((1, 2, tq, LANES), lambda b, p, i: (b, p, i, 0)),
                  pl.BlockSpec((1, 2, S, LANES), lambda b, p, i: (b, p, 0, 0)),
                  pl.BlockSpec((1, S, LANES), lambda b, p, i: (b, 0, p))],
        out_specs=pl.BlockSpec((1, tq, LANES), lambda b, p, i: (b, i, p)),
        out_shape=jax.ShapeDtypeStruct((B, S, MLA_WIDTH), BF16),
        compiler_params=pltpu.CompilerParams(
            dimension_semantics=("parallel", "parallel", "arbitrary"),
            vmem_limit_bytes=VMEM_LIMIT),
        name="mla_attn",
    )(q, k, v)


def _dil_prep_kernel(x_ref, ng_ref, w_ref, g_ref, ones_ref, cos_ref, sin_ref, o_ref, h_scr):
    c = pl.program_id(2)

    @pl.when(c == 0)
    def _():
        h_scr[...] = _rms(x_ref[0], ng_ref[...]).astype(BF16)

    z = jnp.dot(h_scr[...], w_ref[0], preferred_element_type=F32)

    @pl.when(c % 3 != 2)
    def _():
        reps = DIL_GW // LANES
        cos = jnp.concatenate([cos_ref[0]] * reps, axis=-1)
        sin = jnp.concatenate([sin_ref[0]] * reps, axis=-1)
        o_ref[0, 0] = _head_rms_rope(z, ones_ref, DIL_HD, g_ref[0], cos, sin).astype(BF16)

    @pl.when(c % 3 == 2)
    def _():
        o_ref[0, 0] = z.astype(BF16)


def _dil_prep_call(x, ng, wd, gd, ones, cos_t, sin_t, tm):
    B, S, _ = x.shape
    nch = wd.shape[0]
    return pl.pallas_call(
        _dil_prep_kernel,
        grid=(B, S // tm, nch),
        in_specs=[pl.BlockSpec((1, tm, D_MODEL), lambda b, i, c: (b, i, 0)),
                  pl.BlockSpec((1, D_MODEL), lambda b, i, c: (0, 0)),
                  pl.BlockSpec((1, D_MODEL, DIL_GW), lambda b, i, c: (c, 0, 0)),
                  pl.BlockSpec((1, 1, DIL_GW), lambda b, i, c: (c, 0, 0)),
                  pl.BlockSpec((2 * LANES, 2 * LANES), lambda b, i, c: (0, 0)),
                  pl.BlockSpec((1, tm, LANES), lambda b, i, c: (b, i, 0)),
                  pl.BlockSpec((1, tm, LANES), lambda b, i, c: (b, i, 0))],
        out_specs=pl.BlockSpec((1, 1, tm, DIL_GW), lambda b, i, c: (c, b, i, 0)),
        out_shape=jax.ShapeDtypeStruct((nch, B, S, DIL_GW), BF16),
        scratch_shapes=[pltpu.VMEM((tm, D_MODEL), BF16)],
        compiler_params=pltpu.CompilerParams(
            dimension_semantics=("parallel", "parallel", "arbitrary"),
            vmem_limit_bytes=VMEM_LIMIT),
        name="dil_prep",
    )(x, ng, wd, gd, ones, cos_t, sin_t)


def _dil_attn_kernel(q_ref, kc_ref, kp_ref, vc_ref, vp_ref, o_ref, lse_ref, kbuf, vbuf, *, rows):
    i = pl.program_id(2)
    nk = DIL_NK
    kbuf[0:nk, :] = kp_ref[0, 0]
    kbuf[nk:nk + rows, :] = kc_ref[0, 0]
    vbuf[0:nk, :] = vp_ref[0, 0]
    vbuf[nk:nk + rows, :] = vc_ref[0, 0]

    qi = lax.broadcasted_iota(jnp.int32, (2 * nk, 2 * nk), 0) % nk
    ki = lax.broadcasted_iota(jnp.int32, (2 * nk, 2 * nk), 1)
    band = (ki >= qi) & (ki <= qi + nk)
    band_first = band & (ki >= jnp.where(i > 0, 0, nk))
    lane = lax.broadcasted_iota(jnp.int32, (nk, LANES), 1)
    head_a = (lane // (DIL_HD // 2)) % 2 == 0
    lane_lse = lax.broadcasted_iota(jnp.int32, (nk, LANES), 1)

    for n in range(rows // nk):
        mask = band_first if n == 0 else band
        lse_acc = jnp.zeros((nk, LANES), F32)
        for p in range(DIL_HEADS // 2):
            cs = slice(p * LANES, (p + 1) * LANES)
            qp = q_ref[0, 0, n * nk:(n + 1) * nk, cs]
            zero = jnp.zeros_like(qp)
            q2 = jnp.concatenate([jnp.where(head_a, qp, zero), jnp.where(head_a, zero, qp)], axis=0)
            kk = kbuf[n * nk:(n + 2) * nk, cs]
            vv = vbuf[n * nk:(n + 2) * nk, cs]
            s = lax.dot_general(q2, kk, (((1,), (1,)), ((), ())), preferred_element_type=F32)
            s = jnp.where(mask, s, NEG_INF)
            m = jnp.max(s, axis=-1, keepdims=True)
            e = jnp.exp(s - m)
            den = jnp.sum(e, axis=-1, keepdims=True)
            pv = jnp.dot(e.astype(BF16), vv, preferred_element_type=F32) / den
            lse = m + jnp.log(den)
            o_ref[0, n * nk:(n + 1) * nk, cs] = jnp.where(
                lane < DIL_HD, pv[:nk], pv[nk:]).astype(o_ref.dtype)
            lse_acc = jnp.where(lane_lse == 2 * p, lse[:nk], lse_acc)
            lse_acc = jnp.where(lane_lse == 2 * p + 1, lse[nk:], lse_acc)
        lse_ref[0, n * nk:(n + 1) * nk, :] = lse_acc


def _dil_attn_call(qkv, g, dil, rows):
    nch, B, S, _ = qkv.shape
    M = S // dil
    view = qkv.reshape(nch, B, M, dil * DIL_GW)
    nb = rows // DIL_NK
    cur = lambda ch: pl.BlockSpec((1, 1, rows, DIL_GW), lambda b, r, i: (ch, b, i, r))
    prev = lambda ch: pl.BlockSpec((1, 1, DIL_NK, DIL_GW),
                                   lambda b, r, i: (ch, b, jnp.maximum(i * nb - 1, 0), r))
    o, lse = pl.pallas_call(
        functools.partial(_dil_attn_kernel, rows=rows),
        grid=(B, dil, M // rows),
        in_specs=[cur(3 * g), cur(3 * g + 1), prev(3 * g + 1), cur(3 * g + 2), prev(3 * g + 2)],
        out_specs=[pl.BlockSpec((1, rows, DIL_GW), lambda b, r, i: (b, i, r)),
                   pl.BlockSpec((1, rows, LANES), lambda b, r, i: (b, i, r))],
        out_shape=[jax.ShapeDtypeStruct((B, M, dil * DIL_GW), BF16),
                   jax.ShapeDtypeStruct((B, M, dil * LANES), F32)],
        scratch_shapes=[pltpu.VMEM((rows + DIL_NK, DIL_GW), BF16),
                        pltpu.VMEM((rows + DIL_NK, DIL_GW), BF16)],
        compiler_params=pltpu.CompilerParams(
            dimension_semantics=("parallel", "parallel", "arbitrary"),
            vmem_limit_bytes=VMEM_LIMIT),
        name=f"dil_attn_g{g}",
    )(view, view, view, view, view)
    return o.reshape(B, S, DIL_GW), lse.reshape(B, S, LANES)


def _merge_kernel(x_ref, ylru_ref, omla_ref, od0_ref, od1_ref, od2_ref, l0_ref, l1_ref, l2_ref,
                  ng_ref, wf_ref, bm_ref, plru_ref, pmla_ref, pdil_ref, wout_ref, exp_ref, out_ref):
    x = x_ref[0]
    h = _rms(x, ng_ref[...]).astype(BF16)

    def proj(lo, width):
        return jnp.dot(h, wf_ref[:, lo:lo + width], preferred_element_type=F32)

    g_off = N_BRANCH * D_MODEL
    mla_gate = proj(g_off, MLA_WIDTH)
    dil_gate = proj(g_off + MLA_WIDTH, DIL_WIDTH)
    y_mla = (omla_ref[0].astype(F32) * (mla_gate * jax.nn.sigmoid(mla_gate))).astype(BF16)

    lses = [l0_ref[0], l1_ref[0], l2_ref[0]]
    mx = jnp.maximum(jnp.maximum(lses[0], lses[1]), lses[2])
    es = [jnp.exp(l - mx) for l in lses]
    den = es[0] + es[1] + es[2]
    ods = [od0_ref, od1_ref, od2_ref]
    o_dil = None
    for g in range(len(DIL_GROUPS)):
        w = _split_dot(es[g] / den, exp_ref[...])
        term = w * ods[g][0].astype(F32)
        o_dil = term if o_dil is None else o_dil + term
    y_dil = (o_dil * (dil_gate * jax.nn.sigmoid(dil_gate))).astype(BF16)

    ys = [ylru_ref[0], y_mla, y_dil]
    ps = [plru_ref, pmla_ref, pdil_ref]
    merged = None
    for br in range(N_BRANCH):
        gate = jax.nn.sigmoid(proj(br * D_MODEL, D_MODEL) + bm_ref[:, br * D_MODEL:(br + 1) * D_MODEL])
        term = gate * jnp.dot(ys[br], ps[br][...], preferred_element_type=F32)
        merged = term if merged is None else merged + term
    out_ref[0] = x + jnp.dot(merged.astype(BF16), wout_ref[...], preferred_element_type=F32)


def _merge_call(x, ylru, omla, ods, lses, ng, wf, bm, plru, pmla, pdil, wout, expand, tm):
    B, S, _ = x.shape
    tok = lambda w: pl.BlockSpec((1, tm, w), lambda b, i: (b, i, 0))
    c2 = lambda a: pl.BlockSpec(a.shape, lambda b, i: (0, 0))
    return pl.pallas_call(
        _merge_kernel,
        grid=(B, S // tm),
        in_specs=[tok(D_MODEL), tok(LRU_WIDTH), tok(MLA_WIDTH),
                  tok(DIL_GW), tok(DIL_GW), tok(DIL_GW), tok(LANES), tok(LANES), tok(LANES),
                  c2(ng), c2(wf), c2(bm), c2(plru), c2(pmla), c2(pdil), c2(wout), c2(expand)],
        out_specs=tok(D_MODEL),
        out_shape=jax.ShapeDtypeStruct((B, S, D_MODEL), F32),
        compiler_params=pltpu.CompilerParams(
            dimension_semantics=("parallel", "parallel"), vmem_limit_bytes=VMEM_LIMIT),
        name="merge",
    )(x, ylru, omla, *ods, *lses, ng, wf, bm, plru, pmla, pdil, wout, expand)


def _mla_head_cols(w, head_stride, feat_lo, feat_hi):
    ok = (_MLA_FEAT >= feat_lo) & (_MLA_FEAT < feat_hi)
    idx = np.concatenate([h * head_stride + np.where(ok, _MLA_FEAT, 0) for h in range(MLA_HEADS)])
    keep = np.concatenate([ok] * MLA_HEADS)
    return jnp.where(keep[None, :], w[:, idx], 0.0)


def _mla_gain_lanes(g):
    ok = _MLA_FEAT >= 0
    lane = jnp.where(ok, g[np.where(ok, _MLA_FEAT, 0)], 0.0)
    return jnp.concatenate([lane] * MLA_HEADS)[None, :]


def kernel(x, positions, norm_g, w_in, conv_w, conv_b, w_gate_x, b_gate_x, w_gate_a, b_gate_a,
           lru_lambda, w_lru_o, cq_norm_g, ckv_norm_g, w_uq, w_ukv, mla_q_norm_g, mla_k_norm_g,
           w_mla_o, dil_q_norm_g, dil_k_norm_g, w_dil_o, b_merge, w_out):
    B, S, _ = x.shape
    tm = min(512, S)
    cos_t, sin_t = _rope_tables(positions, tm)
    dil_ones = jnp.asarray(_DIL_ONES, BF16)
    mla_ones = jnp.asarray(_MLA_ONES, BF16)
    expand = jnp.asarray(
        (np.arange(LANES)[:, None] == np.arange(DIL_GW)[None, :] // DIL_HD).astype(np.float32), BF16)

    for l in range(DEPTH):
        wl = w_in[l]
        ng = norm_g[l][None, :]

        y_lru = _lru_call(
            x, ng, wl[:, OFF_LRU_X:OFF_CQ].astype(BF16), conv_w[l], conv_b[l][None, :],
            w_gate_x[l].astype(BF16), b_gate_x[l][:, None, :],
            w_gate_a[l].astype(BF16), b_gate_a[l][:, None, :], lru_lambda[l][None, :], tm)

        kr_ok = _MLA_FEAT >= MLA_NOPE
        w_kr = jnp.where(kr_ok[None, :],
                         wl[:, OFF_KR + np.where(kr_ok, _MLA_FEAT - MLA_NOPE, 0)], 0.0)
        wb = jnp.concatenate([wl[:, OFF_CQ:OFF_KR], w_kr], axis=1).astype(BF16)
        wuq = _mla_head_cols(w_uq[l], MLA_QK, 0, MLA_QK).astype(BF16)
        wuk = _mla_head_cols(w_ukv[l], MLA_NOPE + MLA_V, 0, MLA_NOPE).astype(BF16)
        wuv = w_ukv[l].reshape(KV_LORA, MLA_HEADS, MLA_NOPE + MLA_V)[:, :, MLA_NOPE:]
        wuv = wuv.reshape(KV_LORA, MLA_WIDTH).astype(BF16)
        q, k, v = _mla_prep_call(
            x, ng, wb, cq_norm_g[l][None, :], ckv_norm_g[l][None, :], wuq, wuk, wuv,
            _mla_gain_lanes(mla_q_norm_g[l]), _mla_gain_lanes(mla_k_norm_g[l]),
            mla_ones, cos_t, sin_t, tm)
        o_mla = _mla_attn_call(q, k, v, tm)

        chunks, gains = [], []
        gq = jnp.concatenate([dil_q_norm_g[l][_DIL_FEAT]] * (DIL_GW // LANES)) * (DIL_HD ** -0.5)
        gk = jnp.concatenate([dil_k_norm_g[l][_DIL_FEAT]] * (DIL_GW // LANES))
        for g in range(len(DIL_GROUPS)):
            chunks += [wl[:, OFF_DQ + g * DIL_GW + _DIL_PERM],
                       wl[:, OFF_DK + g * DIL_GW + _DIL_PERM],
                       wl[:, OFF_DV + g * DIL_GW:OFF_DV + (g + 1) * DIL_GW]]
            gains += [gq, gk, jnp.ones_like(gq)]
        wd = jnp.stack(chunks).astype(BF16)
        gd = jnp.stack(gains)[:, None, :]
        qkv = _dil_prep_call(x, ng, wd, gd, dil_ones, cos_t, sin_t, tm)
        ods, lses = [], []
        for g, (window, dil) in enumerate(DIL_GROUPS):
            assert window // dil == DIL_NK
            o_g, lse_g = _dil_attn_call(qkv, g, dil, min(2 * DIL_NK, S // dil))
            ods.append(o_g)
            lses.append(lse_g)

        wf = jnp.concatenate([wl[:, OFF_MERGE:], wl[:, OFF_MLA_G:OFF_DQ], wl[:, OFF_DIL_G:OFF_MERGE]],
                             axis=1).astype(BF16)
        x = _merge_call(x, y_lru, o_mla, ods, lses, ng, wf, b_merge[l][None, :],
                        w_lru_o[l].astype(BF16), w_mla_o[l].astype(BF16), w_dil_o[l].astype(BF16),
                        w_out[l].astype(BF16), expand, tm)
    return x
```

```python
import functools
import math

import numpy as np
import jax
import jax.numpy as jnp
from jax import lax
from jax.experimental import pallas as pl
from jax.experimental.pallas import tpu as pltpu

F32 = jnp.float32
BF16 = jnp.bfloat16

D_MODEL = 1024
DEPTH = 2
EPS = 1e-6
ROPE_THETA = 10000.0

LRU_WIDTH = 1024
LRU_BLOCKS = 8
LRU_BLOCK_W = LRU_WIDTH // LRU_BLOCKS
CONV_WIDTH = 4
LRU_C = 8.0

MLA_HEADS = 8
MLA_NOPE = 64
MLA_ROPE = 32
MLA_QK = MLA_NOPE + MLA_ROPE
MLA_V = 64
Q_LORA = 256
KV_LORA = 128
MLA_WIDTH = MLA_HEADS * MLA_V

DIL_GROUPS = ((128, 1), (512, 4), (2048, 16))
DIL_HEADS = 8
DIL_HD = 64
DIL_GW = DIL_HEADS * DIL_HD
DIL_QKV = len(DIL_GROUPS) * DIL_GW
DIL_WIDTH = DIL_GW
DIL_NK = 128

N_BRANCH = 3
OFF_LRU_X = 0
OFF_LRU_G = OFF_LRU_X + LRU_WIDTH
OFF_CQ = OFF_LRU_G + LRU_WIDTH
OFF_CKV = OFF_CQ + Q_LORA
OFF_KR = OFF_CKV + KV_LORA
OFF_MLA_G = OFF_KR + MLA_ROPE
OFF_DQ = OFF_MLA_G + MLA_WIDTH
OFF_DK = OFF_DQ + DIL_QKV
OFF_DV = OFF_DK + DIL_QKV
OFF_DIL_G = OFF_DV + DIL_QKV
OFF_MERGE = OFF_DIL_G + DIL_WIDTH

LANES = 128
SUBLANES = 8
VMEM_LIMIT = 48 * 1024 * 1024

NEG_INF = float("-inf")
LOG2E = math.log2(math.e)
LN2 = math.log(2.0)


def _dil_pair_perm():
    half = DIL_HD // 2
    cols = []
    for p in range(DIL_HEADS // 2):
        for quarter in range(4):
            head = 2 * p + (quarter % 2)
            cols.append(head * DIL_HD + half * (quarter // 2) + np.arange(half))
    return np.concatenate(cols)


def _mla_lane_feat():
    feat = -np.ones((LANES,), np.int64)
    feat[0:48] = np.arange(48)
    feat[48:64] = MLA_NOPE + np.arange(16)
    feat[64:80] = 48 + np.arange(16)
    feat[112:128] = MLA_NOPE + 16 + np.arange(16)
    return feat


_DIL_PERM = _dil_pair_perm()
_DIL_FEAT = _DIL_PERM[:LANES] % DIL_HD
_MLA_FEAT = _mla_lane_feat()


def _block_ones(width, head_of_lane):
    m = (head_of_lane[:, None] == head_of_lane[None, :]).astype(np.float32)
    assert m.shape == (width, width)
    return m


_DIL_HEAD_OF_LANE = (np.arange(2 * LANES) // LANES) * 2 + (np.arange(2 * LANES) % LANES // 32) % 2
_DIL_ONES = _block_ones(2 * LANES, _DIL_HEAD_OF_LANE)
_MLA_ONES = _block_ones(2 * LANES, np.arange(2 * LANES) // LANES)


def _const_spec(shape):
    nd = len(shape)
    return pl.BlockSpec(shape, lambda *_: (0,) * nd, pipeline_mode=pl.Buffered(1))


def _rms(xf, g_row):
    y = xf * lax.rsqrt(jnp.mean(xf * xf, axis=-1, keepdims=True) + EPS)
    return y * g_row


def _split_dot(a_f32, m_bf16):
    hi = a_f32.astype(BF16)
    lo = (a_f32 - hi.astype(F32)).astype(BF16)
    return (jnp.dot(hi, m_bf16, preferred_element_type=F32)
            + jnp.dot(lo, m_bf16, preferred_element_type=F32))


def _head_rms_rope(z, ones_ref, head_dim, g_row, cos, sin):
    n = z.shape[-1]
    parts = []
    for c in range(n // (2 * LANES)):
        zc = z[:, c * 2 * LANES:(c + 1) * 2 * LANES]
        ss = jnp.dot((zc * zc).astype(BF16), ones_ref[...], preferred_element_type=F32)
        y = zc * lax.rsqrt(ss * (1.0 / head_dim) + EPS) * g_row[:, c * 2 * LANES:(c + 1) * 2 * LANES]
        for s in range(2):
            ys = y[:, s * LANES:(s + 1) * LANES]
            parts.append(ys * cos + pltpu.roll(ys, LANES // 2, 1) * sin)
    return jnp.concatenate(parts, axis=-1)


def _rope_kernel(pos_ref, inv_ref, sgn_ref, cd_ref, sd_ref, cm_ref, sm_ref):
    pos = pos_ref[0].astype(F32)
    for t, (c_ref, s_ref) in enumerate(((cd_ref, sd_ref), (cm_ref, sm_ref))):
        ang = pos * inv_ref[t:t + 1, :]
        c_ref[0] = jnp.cos(ang)
        s_ref[0] = jnp.sin(ang) * sgn_ref[t:t + 1, :]


def _rope_tables(positions, tm):
    B, S = positions.shape
    inv_d = ROPE_THETA ** (-jnp.arange(0, DIL_HD, 2, dtype=F32) / DIL_HD)
    inv_m = ROPE_THETA ** (-jnp.arange(0, MLA_ROPE, 2, dtype=F32) / MLA_ROPE)
    lane = np.arange(LANES)
    inv_d_l = inv_d[lane % (DIL_HD // 2)]
    sgn_d_l = np.where(lane < LANES // 2, -1.0, 1.0)
    rope_feat = _MLA_FEAT - MLA_NOPE
    is_rope = rope_feat >= 0
    inv_m_l = jnp.where(is_rope, inv_m[np.where(is_rope, rope_feat % (MLA_ROPE // 2), 0)], 0.0)
    sgn_m_l = np.where(is_rope, np.where(rope_feat < MLA_ROPE // 2, -1.0, 1.0), 0.0)
    inv = jnp.stack([inv_d_l, inv_m_l]).astype(F32)
    sgn = jnp.asarray(np.stack([sgn_d_l, sgn_m_l]), F32)
    out = jax.ShapeDtypeStruct((B, S, LANES), F32)
    tab = pl.BlockSpec((1, tm, LANES), lambda b, i: (b, i, 0))
    return pl.pallas_call(
        _rope_kernel,
        grid=(B, S // tm),
        in_specs=[pl.BlockSpec((1, tm, 1), lambda b, i: (b, i, 0)),
                  _const_spec((2, LANES)), _const_spec((2, LANES))],
        out_specs=[tab, tab, tab, tab],
        out_shape=[out, out, out, out],
        compiler_params=pltpu.CompilerParams(
            dimension_semantics=("parallel", "parallel"), vmem_limit_bytes=VMEM_LIMIT),
        name="rope_tables",
    )(positions.reshape(B, S, 1), inv, sgn)


def _lru_kernel(x_ref, ng_ref, w_ref, cw_ref, cb_ref, wgx_ref, bgx_ref, wga_ref, bga_ref,
                lam_ref, y_ref, xbuf, a_scr, b_scr, h_scr, hcar, *, ts):
    t = pl.program_id(1)

    @pl.when(t == 0)
    def _():
        xbuf[0:SUBLANES, :] = jnp.zeros((SUBLANES, LRU_WIDTH), F32)
        hcar[...] = jnp.zeros((1, LRU_WIDTH), F32)

    h = _rms(x_ref[0], ng_ref[...]).astype(BF16)
    z = jnp.dot(h, w_ref[...], preferred_element_type=F32)
    lx = z[:, :LRU_WIDTH]
    gate = z[:, LRU_WIDTH:]
    xbuf[SUBLANES:SUBLANES + ts, :] = lx
    xc = cb_ref[...]
    for k in range(CONV_WIDTH):
        lo = SUBLANES - (CONV_WIDTH - 1) + k
        xc = xc + xbuf[lo:lo + ts, :] * cw_ref[k:k + 1, :]
    xbuf[0:SUBLANES, :] = xbuf[ts:ts + SUBLANES, :]

    lam = lam_ref[...]
    neg_sp = -(jnp.maximum(-lam, 0.0) + jnp.log1p(jnp.exp(-jnp.abs(lam))))
    for n in range(LRU_BLOCKS):
        sl = slice(n * LRU_BLOCK_W, (n + 1) * LRU_BLOCK_W)
        xb = xc[:, sl]
        xb16 = xb.astype(BF16)
        gx = jax.nn.sigmoid(jnp.dot(xb16, wgx_ref[n], preferred_element_type=F32) + bgx_ref[n])
        ga = jax.nn.sigmoid(jnp.dot(xb16, wga_ref[n], preferred_element_type=F32) + bga_ref[n])
        log_a = LRU_C * ga * neg_sp[:, sl]
        a = jnp.exp(log_a)
        mult = jnp.sqrt(-jnp.tanh(log_a) * (1.0 + a * a))
        a_scr[:, sl] = a
        b_scr[:, sl] = mult * (gx * xb)

    row = lax.broadcasted_iota(jnp.int32, (SUBLANES, LRU_WIDTH), 0)

    def group(g, carry):
        r = pl.multiple_of(g * SUBLANES, SUBLANES)
        a8 = a_scr[pl.ds(r, SUBLANES), :]
        b8 = b_scr[pl.ds(r, SUBLANES), :]
        for s in (1, 2, 4):
            keep = row >= s
            a_prev = pltpu.roll(a8, s, 0)
            b_prev = pltpu.roll(b8, s, 0)
            b8 = jnp.where(keep, a8 * b_prev + b8, b8)
            a8 = jnp.where(keep, a8 * a_prev, a8)
        h8 = a8 * carry + b8
        h_scr[pl.ds(r, SUBLANES), :] = h8
        return h8[SUBLANES - 1:SUBLANES, :]

    hcar[...] = lax.fori_loop(0, ts // SUBLANES, group, hcar[...], unroll=2)
    y_ref[0] = (h_scr[...] * (gate * jax.nn.sigmoid(gate))).astype(BF16)


def _lru_call(x, ng, w, cw, cb, wgx, bgx, wga, bga, lam, ts):
    B, S, _ = x.shape
    return pl.pallas_call(
        functools.partial(_lru_kernel, ts=ts),
        grid=(B, S // ts),
        in_specs=[pl.BlockSpec((1, ts, D_MODEL), lambda b, t: (b, t, 0)),
                  _const_spec(ng.shape), _const_spec(w.shape), _const_spec(cw.shape),
                  _const_spec(cb.shape), _const_spec(wgx.shape), _const_spec(bgx.shape),
                  _const_spec(wga.shape), _const_spec(bga.shape), _const_spec(lam.shape)],
        out_specs=pl.BlockSpec((1, ts, LRU_WIDTH), lambda b, t: (b, t, 0)),
        out_shape=jax.ShapeDtypeStruct((B, S, LRU_WIDTH), BF16),
        scratch_shapes=[pltpu.VMEM((ts + SUBLANES, LRU_WIDTH), F32),
                        pltpu.VMEM((ts, LRU_WIDTH), F32),
                        pltpu.VMEM((ts, LRU_WIDTH), F32),
                        pltpu.VMEM((ts, LRU_WIDTH), F32),
                        pltpu.VMEM((1, LRU_WIDTH), F32)],
        compiler_params=pltpu.CompilerParams(
            dimension_semantics=("parallel", "arbitrary"), vmem_limit_bytes=VMEM_LIMIT),
        name="lru_branch",
    )(x, ng, w, cw, cb, wgx, bgx, wga, bga, lam)


def _mla_prep_kernel(x_ref, ng_ref, wb_ref, gcq_ref, gckv_ref, wuq_ref, wuk_ref, wuv_ref,
                     gq_ref, gk_ref, ones_ref, vone_ref, cos_ref, sin_ref, q_ref, k_ref, v_ref):
    h = _rms(x_ref[0], ng_ref[...]).astype(BF16)
    z = jnp.dot(h, wb_ref[...], preferred_element_type=F32)
    cq = _rms(z[:, :Q_LORA], gcq_ref[...]).astype(BF16)
    ckv = _rms(z[:, Q_LORA:Q_LORA + KV_LORA], gckv_ref[...]).astype(BF16)
    kr = z[:, Q_LORA + KV_LORA:]
    cos = cos_ref[0]
    sin = sin_ref[0]
    q = jnp.dot(cq, wuq_ref[...], preferred_element_type=F32)
    q = _head_rms_rope(q, ones_ref, MLA_QK, gq_ref[...], cos, sin) * (MLA_QK ** -0.5 * LOG2E)
    k = jnp.dot(ckv, wuk_ref[...], preferred_element_type=F32)
    k = k + jnp.concatenate([kr] * MLA_HEADS, axis=-1)
    k = _head_rms_rope(k, ones_ref, MLA_QK, gk_ref[...], cos, sin)
    for hh in range(MLA_HEADS):
        q_ref[0, hh] = q[:, hh * LANES:(hh + 1) * LANES].astype(BF16)
        k_ref[0, hh] = k[:, hh * LANES:(hh + 1) * LANES].astype(BF16)
    v = jnp.dot(ckv, wuv_ref[...], preferred_element_type=F32) + vone_ref[...]
    for hh in range(MLA_HEADS):
        v_ref[0, hh] = v[:, hh * LANES:(hh + 1) * LANES].astype(BF16)


def _mla_prep_call(x, ng, wb, gcq, gckv, wuq, wuk, wuv, gq, gk, ones, vone, cos_t, sin_t, tm):
    B, S, _ = x.shape
    tab = pl.BlockSpec((1, tm, LANES), lambda b, i: (b, i, 0))
    consts = (ng, wb, gcq, gckv, wuq, wuk, wuv, gq, gk, ones, vone)
    return pl.pallas_call(
        _mla_prep_kernel,
        grid=(B, S // tm),
        in_specs=[pl.BlockSpec((1, tm, D_MODEL), lambda b, i: (b, i, 0))]
        + [_const_spec(a.shape) for a in consts] + [tab, tab],
        out_specs=[pl.BlockSpec((1, MLA_HEADS, tm, LANES), lambda b, i: (b, 0, i, 0))] * 3,
        out_shape=[jax.ShapeDtypeStruct((B, MLA_HEADS, S, LANES), BF16)] * 3,
        compiler_params=pltpu.CompilerParams(
            dimension_semantics=("parallel", "parallel"), vmem_limit_bytes=VMEM_LIMIT),
        name="mla_prep",
    )(x, *consts, cos_t, sin_t)


def _mla_attn_kernel(q_ref, k_ref, v_ref, o_ref, *, tq):
    qi = pl.program_id(2)
    row = lax.broadcasted_iota(jnp.int32, (tq, tq), 0)
    col = lax.broadcasted_iota(jnp.int32, (tq, tq), 1)
    causal = col <= row

    def step(j, carry, masked):
        start = pl.multiple_of(j * tq, tq)
        new = []
        for hh in range(2):
            m, acc = carry[hh]
            k = k_ref[0, hh, pl.ds(start, tq), :]
            v = v_ref[0, hh, pl.ds(start, tq), :]
            s = lax.dot_general(q_ref[0, hh], k, (((1,), (1,)), ((), ())),
                                preferred_element_type=F32)
            if masked:
                s = jnp.where(causal, s, NEG_INF)
            m_new = jnp.maximum(m, jnp.max(s, axis=-1, keepdims=True))
            p = jnp.exp2((s - m_new).astype(BF16))
            acc = jnp.exp2(m - m_new) * acc + jnp.dot(p, v, preferred_element_type=F32)
            new.append((m_new, acc))
        return tuple(new)

    init = tuple((jnp.full((tq, 1), NEG_INF, F32), jnp.zeros((tq, LANES), F32)) for _ in range(2))
    carry = lax.fori_loop(0, qi, functools.partial(step, masked=False), init)
    (_, acc0), (_, acc1) = step(qi, carry, True)
    lane = lax.broadcasted_iota(jnp.int32, (tq, LANES), 1)
    o = jnp.where(lane < MLA_V, acc0 / pltpu.roll(acc0, MLA_V, 1), acc1 / pltpu.roll(acc1, MLA_V, 1))
    o_ref[0] = o.astype(BF16)


def _mla_attn_call(q, k, v, tq):
    B, H, S, _ = q.shape
    return pl.pallas_call(
        functools.partial(_mla_attn_kernel, tq=tq),
        grid=(B, H // 2, S // tq),
        in_specs=[pl.BlockSpec((1, 2, tq, LANES), lambda b, p, i: (b, p, i, 0)),
                  pl.BlockSpec((1, 2, S, LANES), lambda b, p, i: (b, p, 0, 0)),
                  pl.BlockSpec((1, 2, S, LANES), lambda b, p, i: (b, p, 0, 0))],
        out_specs=pl.BlockSpec((1, tq, LANES), lambda b, p, i: (b, i, p)),
        out_shape=jax.ShapeDtypeStruct((B, S, MLA_WIDTH), BF16),
        compiler_params=pltpu.CompilerParams(
            dimension_semantics=("parallel", "parallel", "arbitrary"),
            vmem_limit_bytes=VMEM_LIMIT),
        name="mla_attn",
    )(q, k, v)


def _residue_major(ref_at, dil, n):
    if dil == 1:
        return ref_at[...]
    return jnp.concatenate([ref_at[pl.ds(r, n, stride=dil), :] for r in range(dil)], axis=0)


def _dil_prep_kernel(x_ref, ng_ref, w_ref, g_ref, ones_ref, cos_ref, sin_ref,
                     o0_ref, o1_ref, o2_ref, xs_scr, *, tm):
    xn = _rms(x_ref[0], ng_ref[...])
    nslab = D_MODEL // LANES
    for c in range(nslab):
        xs_scr[c] = xn[:, c * LANES:(c + 1) * LANES]
    outs = (o0_ref, o1_ref, o2_ref)
    for g, (_, dil) in enumerate(DIL_GROUPS):
        n = tm // dil
        h = jnp.concatenate([_residue_major(xs_scr.at[c], dil, n) for c in range(nslab)],
                            axis=-1).astype(BF16)
        cos = _residue_major(cos_ref.at[0], dil, n)
        sin = _residue_major(sin_ref.at[0], dil, n)
        for j in range(3):
            z = jnp.dot(h, w_ref[3 * g + j], preferred_element_type=F32)
            if j < 2:
                z = _head_rms_rope(z, ones_ref, DIL_HD, g_ref[3 * g + j], cos, sin)
            if j == 0:
                z = z * (DIL_HD ** -0.5 * LOG2E)
            z = z.astype(BF16)
            for r in range(dil):
                outs[g][j, 0, r] = z[r * n:(r + 1) * n]


def _dil_prep_call(x, ng, wd, gd, ones, cos_t, sin_t, tm):
    B, S, _ = x.shape
    tab = pl.BlockSpec((1, tm, LANES), lambda b, i: (b, i, 0))
    out_specs, out_shape = [], []
    for _, dil in DIL_GROUPS:
        out_specs.append(pl.BlockSpec((3, 1, dil, tm // dil, DIL_GW), lambda b, i: (0, b, 0, i, 0)))
        out_shape.append(jax.ShapeDtypeStruct((3, B, dil, S // dil, DIL_GW), BF16))
    return pl.pallas_call(
        functools.partial(_dil_prep_kernel, tm=tm),
        grid=(B, S // tm),
        in_specs=[pl.BlockSpec((1, tm, D_MODEL), lambda b, i: (b, i, 0)),
                  _const_spec(ng.shape), _const_spec(wd.shape), _const_spec(gd.shape),
                  _const_spec(ones.shape), tab, tab],
        out_specs=out_specs,
        out_shape=out_shape,
        scratch_shapes=[pltpu.VMEM((D_MODEL // LANES, tm, LANES), F32)],
        compiler_params=pltpu.CompilerParams(
            dimension_semantics=("parallel", "parallel"), vmem_limit_bytes=VMEM_LIMIT),
        name="dil_prep",
    )(x, ng, wd, gd, ones, cos_t, sin_t)


def _dil_attn_kernel(q_ref, kc_ref, kp_ref, vc_ref, vp_ref, o_ref, lse_ref, kbuf, vbuf, *, rows):
    i = pl.program_id(2)
    nk = DIL_NK
    kbuf[0:nk, :] = kp_ref[0, 0, 0]
    kbuf[nk:nk + rows, :] = kc_ref[0, 0, 0]
    vbuf[0:nk, :] = vp_ref[0, 0, 0]
    vbuf[nk:nk + rows, :] = vc_ref[0, 0, 0]

    qi = lax.broadcasted_iota(jnp.int32, (2 * nk, 2 * nk), 0) % nk
    ki = lax.broadcasted_iota(jnp.int32, (2 * nk, 2 * nk), 1)
    band = (ki >= qi) & (ki <= qi + nk)
    band_first = band & (ki >= jnp.where(i > 0, 0, nk))
    lane = lax.broadcasted_iota(jnp.int32, (nk, LANES), 1)
    head_a = (lane // (DIL_HD // 2)) % 2 == 0

    for n in range(rows // nk):
        mask = band_first if n == 0 else band
        lse_acc = jnp.zeros((nk, LANES), F32)
        for p in range(DIL_HEADS // 2):
            cs = slice(p * LANES, (p + 1) * LANES)
            qp = q_ref[0, 0, 0, n * nk:(n + 1) * nk, cs]
            zero = jnp.zeros_like(qp)
            q2 = jnp.concatenate([jnp.where(head_a, qp, zero), jnp.where(head_a, zero, qp)], axis=0)
            kk = kbuf[n * nk:(n + 2) * nk, cs]
            vv = vbuf[n * nk:(n + 2) * nk, cs]
            s = lax.dot_general(q2, kk, (((1,), (1,)), ((), ())), preferred_element_type=F32)
            s = jnp.where(mask, s, NEG_INF)
            m = jnp.max(s, axis=-1, keepdims=True)
            e = jnp.exp2(s - m)
            den = jnp.sum(e, axis=-1, keepdims=True)
            pv = jnp.dot(e.astype(BF16), vv, preferred_element_type=F32) / den
            lse = (m + jnp.log2(den)) * LN2
            o_ref[0, 0, n * nk:(n + 1) * nk, cs] = jnp.where(
                lane < DIL_HD, pv[:nk], pv[nk:]).astype(o_ref.dtype)
            lse_acc = jnp.where(lane == 2 * p, lse[:nk], lse_acc)
            lse_acc = jnp.where(lane == 2 * p + 1, lse[nk:], lse_acc)
        lse_ref[0, 0, n * nk:(n + 1) * nk, :] = lse_acc


def _dil_attn_call(qkv, g, rows):
    _, B, dil, M, _ = qkv.shape
    nb = rows // DIL_NK
    cur = lambda ch: pl.BlockSpec((1, 1, 1, rows, DIL_GW), lambda b, r, i: (ch, b, r, i, 0))
    prev = lambda ch: pl.BlockSpec((1, 1, 1, DIL_NK, DIL_GW),
                                   lambda b, r, i: (ch, b, r, jnp.maximum(i * nb - 1, 0), 0))
    return pl.pallas_call(
        functools.partial(_dil_attn_kernel, rows=rows),
        grid=(B, dil, M // rows),
        in_specs=[cur(0), cur(1), prev(1), cur(2), prev(2)],
        out_specs=[pl.BlockSpec((1, 1, rows, DIL_GW), lambda b, r, i: (b, r, i, 0)),
                   pl.BlockSpec((1, 1, rows, LANES), lambda b, r, i: (b, r, i, 0))],
        out_shape=[jax.ShapeDtypeStruct((B, dil, M, DIL_GW), BF16),
                   jax.ShapeDtypeStruct((B, dil, M, LANES), F32)],
        scratch_shapes=[pltpu.VMEM((rows + DIL_NK, DIL_GW), BF16),
                        pltpu.VMEM((rows + DIL_NK, DIL_GW), BF16)],
        compiler_params=pltpu.CompilerParams(
            dimension_semantics=("parallel", "parallel", "arbitrary"),
            vmem_limit_bytes=VMEM_LIMIT),
        name=f"dil_attn_g{g}",
    )(qkv, qkv, qkv, qkv, qkv)


def _token_major(ref, dil, scr):
    n = ref.shape[2]
    nslab = ref.shape[3] // LANES
    if dil == 1:
        return ref[0, 0].astype(F32)
    for r in range(dil):
        blk = ref[0, r].astype(F32)
        for c in range(nslab):
            scr[c, pl.ds(r, n, stride=dil), :] = blk[:, c * LANES:(c + 1) * LANES]
    return jnp.concatenate([scr[c] for c in range(nslab)], axis=-1)


def _merge_kernel(x_ref, ylru_ref, omla_ref, od0_ref, od1_ref, od2_ref, l0_ref, l1_ref, l2_ref,
                  ng_ref, wf_ref, bm_ref, plru_ref, pmla_ref, pdil_ref, wout_ref, exp_ref, out_ref,
                  o_scr, l_scr):
    x = x_ref[0]
    h = _rms(x, ng_ref[...]).astype(BF16)

    def proj(lo, width):
        return jnp.dot(h, wf_ref[:, lo:lo + width], preferred_element_type=F32)

    g_off = N_BRANCH * D_MODEL
    mla_gate = proj(g_off, MLA_WIDTH)
    dil_gate = proj(g_off + MLA_WIDTH, DIL_WIDTH)
    y_mla = (omla_ref[0].astype(F32) * (mla_gate * jax.nn.sigmoid(mla_gate))).astype(BF16)

    od_refs = (od0_ref, od1_ref, od2_ref)
    l_refs = (l0_ref, l1_ref, l2_ref)
    lses = [_token_major(l_refs[g], dil, l_scr.at[g]) for g, (_, dil) in enumerate(DIL_GROUPS)]
    mx = jnp.maximum(jnp.maximum(lses[0], lses[1]), lses[2])
    es = [jnp.exp(l - mx) for l in lses]
    den = es[0] + es[1] + es[2]
    o_dil = None
    for g, (_, dil) in enumerate(DIL_GROUPS):
        w = _split_dot(es[g] / den, exp_ref[...])
        term = w * _token_major(od_refs[g], dil, o_scr.at[g])
        o_dil = term if o_dil is None else o_dil + term
    y_dil = (o_dil * (dil_gate * jax.nn.sigmoid(dil_gate))).astype(BF16)

    ys = [ylru_ref[0], y_mla, y_dil]
    ps = [plru_ref, pmla_ref, pdil_ref]
    merged = None
    for br in range(N_BRANCH):
        gate = jax.nn.sigmoid(proj(br * D_MODEL, D_MODEL) + bm_ref[:, br * D_MODEL:(br + 1) * D_MODEL])
        term = gate * jnp.dot(ys[br], ps[br][...], preferred_element_type=F32)
        merged = term if merged is None else merged + term
    out_ref[0] = x + jnp.dot(merged.astype(BF16), wout_ref[...], preferred_element_type=F32)


def _merge_call(x, ylru, omla, ods, lses, ng, wf, bm, plru, pmla, pdil, wout, expand, tm):
    B, S, _ = x.shape
    tok = lambda w: pl.BlockSpec((1, tm, w), lambda b, i: (b, i, 0))
    res = lambda dil, w: pl.BlockSpec((1, dil, tm // dil, w), lambda b, i: (b, 0, i, 0))
    dils = [dil for _, dil in DIL_GROUPS]
    consts = (ng, wf, bm, plru, pmla, pdil, wout, expand)
    ngrp = len(DIL_GROUPS)
    return pl.pallas_call(
        _merge_kernel,
        grid=(B, S // tm),
        in_specs=[tok(D_MODEL), tok(LRU_WIDTH), tok(MLA_WIDTH)]
        + [res(d, DIL_GW) for d in dils] + [res(d, LANES) for d in dils]
        + [_const_spec(a.shape) for a in consts],
        out_specs=tok(D_MODEL),
        out_shape=jax.ShapeDtypeStruct((B, S, D_MODEL), F32),
        scratch_shapes=[pltpu.VMEM((ngrp, DIL_GW // LANES, tm, LANES), F32),
                        pltpu.VMEM((ngrp, 1, tm, LANES), F32)],
        compiler_params=pltpu.CompilerParams(
            dimension_semantics=("parallel", "parallel"), vmem_limit_bytes=VMEM_LIMIT),
        name="merge",
    )(x, ylru, omla, *ods, *lses, *consts)


def _mla_head_cols(w, head_stride, feat_lo, feat_hi):
    ok = (_MLA_FEAT >= feat_lo) & (_MLA_FEAT < feat_hi)
    idx = np.concatenate([h * head_stride + np.where(ok, _MLA_FEAT, 0) for h in range(MLA_HEADS)])
    keep = np.concatenate([ok] * MLA_HEADS)
    return jnp.where(keep[None, :], w[:, idx], 0.0)


def _mla_gain_lanes(g):
    ok = _MLA_FEAT >= 0
    lane = jnp.where(ok, g[np.where(ok, _MLA_FEAT, 0)], 0.0)
    return jnp.concatenate([lane] * MLA_HEADS)[None, :]


def kernel(x, positions, norm_g, w_in, conv_w, conv_b, w_gate_x, b_gate_x, w_gate_a, b_gate_a,
           lru_lambda, w_lru_o, cq_norm_g, ckv_norm_g, w_uq, w_ukv, mla_q_norm_g, mla_k_norm_g,
           w_mla_o, dil_q_norm_g, dil_k_norm_g, w_dil_o, b_merge, w_out):
    B, S, _ = x.shape
    tm = min(512, S)
    cos_d, sin_d, cos_m, sin_m = _rope_tables(positions, tm)
    dil_ones = jnp.asarray(_DIL_ONES, BF16)
    mla_ones = jnp.asarray(_MLA_ONES, BF16)
    v_one = jnp.asarray(((np.arange(MLA_HEADS * LANES) // MLA_V + 1) // 2 % 2)[None, :], F32)
    expand = jnp.asarray(
        (np.arange(LANES)[:, None] == np.arange(DIL_GW)[None, :] // DIL_HD).astype(np.float32), BF16)

    for l in range(DEPTH):
        wl = w_in[l]
        ng = norm_g[l][None, :]

        y_lru = _lru_call(
            x, ng, wl[:, OFF_LRU_X:OFF_CQ].astype(BF16), conv_w[l], conv_b[l][None, :],
            w_gate_x[l].astype(BF16), b_gate_x[l][:, None, :],
            w_gate_a[l].astype(BF16), b_gate_a[l][:, None, :], lru_lambda[l][None, :], tm)

        kr_ok = _MLA_FEAT >= MLA_NOPE
        w_kr = jnp.where(kr_ok[None, :],
                         wl[:, OFF_KR + np.where(kr_ok, _MLA_FEAT - MLA_NOPE, 0)], 0.0)
        wb = jnp.concatenate([wl[:, OFF_CQ:OFF_KR], w_kr], axis=1).astype(BF16)
        wuq = _mla_head_cols(w_uq[l], MLA_QK, 0, MLA_QK).astype(BF16)
        wuk = _mla_head_cols(w_ukv[l], MLA_NOPE + MLA_V, 0, MLA_NOPE).astype(BF16)
        wuv = w_ukv[l].reshape(KV_LORA, MLA_HEADS // 2, 2, MLA_NOPE + MLA_V)[..., MLA_NOPE:]
        zv = jnp.zeros_like(wuv[:, :, 0])
        wuv = jnp.stack([wuv[:, :, 0], zv, zv, wuv[:, :, 1]], axis=2)
        wuv = wuv.reshape(KV_LORA, MLA_HEADS * LANES).astype(BF16)
        q, k, v = _mla_prep_call(
            x, ng, wb, cq_norm_g[l][None, :], ckv_norm_g[l][None, :], wuq, wuk, wuv,
            _mla_gain_lanes(mla_q_norm_g[l]), _mla_gain_lanes(mla_k_norm_g[l]),
            mla_ones, v_one, cos_m, sin_m, tm)
        o_mla = _mla_attn_call(q, k, v, tm)

        chunks, gains = [], []
        gq = jnp.concatenate([dil_q_norm_g[l][_DIL_FEAT]] * (DIL_GW // LANES))
        gk = jnp.concatenate([dil_k_norm_g[l][_DIL_FEAT]] * (DIL_GW // LANES))
        for g in range(len(DIL_GROUPS)):
            chunks += [wl[:, OFF_DQ + g * DIL_GW + _DIL_PERM],
                       wl[:, OFF_DK + g * DIL_GW + _DIL_PERM],
                       wl[:, OFF_DV + g * DIL_GW:OFF_DV + (g + 1) * DIL_GW]]
            gains += [gq, gk, jnp.ones_like(gq)]
        wd = jnp.stack(chunks).astype(BF16)
        gd = jnp.stack(gains)[:, None, :]
        qkvs = _dil_prep_call(x, ng, wd, gd, dil_ones, cos_d, sin_d, tm)
        ods, lses = [], []
        for g, (window, dil) in enumerate(DIL_GROUPS):
            assert window // dil == DIL_NK
            o_g, lse_g = _dil_attn_call(qkvs[g], g, min(2 * DIL_NK, S // dil))
            ods.append(o_g)
            lses.append(lse_g)

        wf = jnp.concatenate([wl[:, OFF_MERGE:], wl[:, OFF_MLA_G:OFF_DQ], wl[:, OFF_DIL_G:OFF_MERGE]],
                             axis=1).astype(BF16)
        x = _merge_call(x, y_lru, o_mla, ods, lses, ng, wf, b_merge[l][None, :],
                        w_lru_o[l].astype(BF16), w_mla_o[l].astype(BF16), w_dil_o[l].astype(BF16),
                        w_out[l].astype(BF16), expand, tm)
    return x
```

```python
import functools
import math

import numpy as np
import jax
import jax.numpy as jnp
from jax import lax
from jax.experimental import pallas as pl
from jax.experimental.pallas import tpu as pltpu

F32 = jnp.float32
BF16 = jnp.bfloat16

D_MODEL = 1024
DEPTH = 2
EPS = 1e-6
ROPE_THETA = 10000.0

LRU_WIDTH = 1024
LRU_BLOCKS = 8
LRU_BLOCK_W = LRU_WIDTH // LRU_BLOCKS
CONV_WIDTH = 4
LRU_C = 8.0

MLA_HEADS = 8
MLA_NOPE = 64
MLA_ROPE = 32
MLA_QK = MLA_NOPE + MLA_ROPE
MLA_V = 64
Q_LORA = 256
KV_LORA = 128
MLA_WIDTH = MLA_HEADS * MLA_V

DIL_GROUPS = ((128, 1), (512, 4), (2048, 16))
DIL_HEADS = 8
DIL_HD = 64
DIL_GW = DIL_HEADS * DIL_HD
DIL_QKV = len(DIL_GROUPS) * DIL_GW
DIL_WIDTH = DIL_GW
DIL_NK = 128

N_BRANCH = 3
OFF_LRU_X = 0
OFF_LRU_G = OFF_LRU_X + LRU_WIDTH
OFF_CQ = OFF_LRU_G + LRU_WIDTH
OFF_CKV = OFF_CQ + Q_LORA
OFF_KR = OFF_CKV + KV_LORA
OFF_MLA_G = OFF_KR + MLA_ROPE
OFF_DQ = OFF_MLA_G + MLA_WIDTH
OFF_DK = OFF_DQ + DIL_QKV
OFF_DV = OFF_DK + DIL_QKV
OFF_DIL_G = OFF_DV + DIL_QKV
OFF_MERGE = OFF_DIL_G + DIL_WIDTH

LANES = 128
SUBLANES = 8
VMEM_LIMIT = 48 * 1024 * 1024

NEG_INF = float("-inf")
LOG2E = math.log2(math.e)
LN2 = math.log(2.0)


def _dil_pair_perm():
    half = DIL_HD // 2
    cols = []
    for p in range(DIL_HEADS // 2):
        for quarter in range(4):
            head = 2 * p + (quarter % 2)
            cols.append(head * DIL_HD + half * (quarter // 2) + np.arange(half))
    return np.concatenate(cols)


def _mla_lane_feat():
    feat = -np.ones((LANES,), np.int64)
    feat[0:48] = np.arange(48)
    feat[48:64] = MLA_NOPE + np.arange(16)
    feat[64:80] = 48 + np.arange(16)
    feat[112:128] = MLA_NOPE + 16 + np.arange(16)
    return feat


_DIL_PERM = _dil_pair_perm()
_DIL_FEAT = _DIL_PERM[:LANES] % DIL_HD
_MLA_FEAT = _mla_lane_feat()


def _block_ones(width, head_of_lane):
    m = (head_of_lane[:, None] == head_of_lane[None, :]).astype(np.float32)
    assert m.shape == (width, width)
    return m


_DIL_HEAD_OF_LANE = (np.arange(2 * LANES) // LANES) * 2 + (np.arange(2 * LANES) % LANES // 32) % 2
_DIL_ONES = _block_ones(2 * LANES, _DIL_HEAD_OF_LANE)
_MLA_ONES = _block_ones(2 * LANES, np.arange(2 * LANES) // LANES)


def _const_spec(shape):
    nd = len(shape)
    return pl.BlockSpec(shape, lambda *_: (0,) * nd, pipeline_mode=pl.Buffered(1))


def _rms(xf, g_row):
    y = xf * lax.rsqrt(jnp.mean(xf * xf, axis=-1, keepdims=True) + EPS)
    return y * g_row


def _split_dot(a_f32, m_bf16):
    hi = a_f32.astype(BF16)
    lo = (a_f32 - hi.astype(F32)).astype(BF16)
    return (jnp.dot(hi, m_bf16, preferred_element_type=F32)
            + jnp.dot(lo, m_bf16, preferred_element_type=F32))


def _head_rms_rope(z, ones_ref, head_dim, g_row, cos, sin):
    n = z.shape[-1]
    parts = []
    for c in range(n // (2 * LANES)):
        zc = z[:, c * 2 * LANES:(c + 1) * 2 * LANES]
        ss = jnp.dot((zc * zc).astype(BF16), ones_ref[...], preferred_element_type=F32)
        y = zc * lax.rsqrt(ss * (1.0 / head_dim) + EPS) * g_row[:, c * 2 * LANES:(c + 1) * 2 * LANES]
        for s in range(2):
            ys = y[:, s * LANES:(s + 1) * LANES]
            parts.append(ys * cos + pltpu.roll(ys, LANES // 2, 1) * sin)
    return jnp.concatenate(parts, axis=-1)


def _rope_kernel(pos_ref, inv_ref, sgn_ref, cd_ref, sd_ref, cm_ref, sm_ref):
    pos = pos_ref[0].astype(F32)
    for t, (c_ref, s_ref) in enumerate(((cd_ref, sd_ref), (cm_ref, sm_ref))):
        ang = pos * inv_ref[t:t + 1, :]
        c_ref[0] = jnp.cos(ang)
        s_ref[0] = jnp.sin(ang) * sgn_ref[t:t + 1, :]


def _rope_tables(positions, tm):
    B, S = positions.shape
    inv_d = ROPE_THETA ** (-jnp.arange(0, DIL_HD, 2, dtype=F32) / DIL_HD)
    inv_m = ROPE_THETA ** (-jnp.arange(0, MLA_ROPE, 2, dtype=F32) / MLA_ROPE)
    lane = np.arange(LANES)
    inv_d_l = inv_d[lane % (DIL_HD // 2)]
    sgn_d_l = np.where(lane < LANES // 2, -1.0, 1.0)
    rope_feat = _MLA_FEAT - MLA_NOPE
    is_rope = rope_feat >= 0
    inv_m_l = jnp.where(is_rope, inv_m[np.where(is_rope, rope_feat % (MLA_ROPE // 2), 0)], 0.0)
    sgn_m_l = np.where(is_rope, np.where(rope_feat < MLA_ROPE // 2, -1.0, 1.0), 0.0)
    inv = jnp.stack([inv_d_l, inv_m_l]).astype(F32)
    sgn = jnp.asarray(np.stack([sgn_d_l, sgn_m_l]), F32)
    out = jax.ShapeDtypeStruct((B, S, LANES), F32)
    tab = pl.BlockSpec((1, tm, LANES), lambda b, i: (b, i, 0))
    return pl.pallas_call(
        _rope_kernel,
        grid=(B, S // tm),
        in_specs=[pl.BlockSpec((1, tm, 1), lambda b, i: (b, i, 0)),
                  _const_spec((2, LANES)), _const_spec((2, LANES))],
        out_specs=[tab, tab, tab, tab],
        out_shape=[out, out, out, out],
        compiler_params=pltpu.CompilerParams(
            dimension_semantics=("parallel", "parallel"), vmem_limit_bytes=VMEM_LIMIT),
        name="rope_tables",
    )(positions.reshape(B, S, 1), inv, sgn)


def _lru_kernel(x_ref, ng_ref, w_ref, cw_ref, cb_ref, wgx_ref, bgx_ref, wga_ref, bga_ref,
                lam_ref, perm_ref, unperm_ref, y_ref, xbuf, a_scr, b_scr, tail, hcar, *, ts):
    t = pl.program_id(1)
    nchunk = SUBLANES
    clen = ts // nchunk
    halo = (CONV_WIDTH - 1) * SUBLANES

    @pl.when(t == 0)
    def _():
        tail[...] = jnp.zeros(tail.shape, F32)
        hcar[...] = jnp.zeros((1, LRU_WIDTH), F32)

    h = _rms(x_ref[0], ng_ref[...]).astype(BF16)
    h = jnp.dot(perm_ref[...], h, preferred_element_type=F32).astype(BF16)
    z = jnp.dot(h, w_ref[...], preferred_element_type=F32)
    lx = z[:, :LRU_WIDTH]
    gate = z[:, LRU_WIDTH:]
    xbuf[halo:halo + ts, :] = lx
    last = lx[ts - halo:, :]
    shifted = jnp.concatenate(
        [pltpu.roll(last[g * SUBLANES:(g + 1) * SUBLANES], 1, 0) for g in range(CONV_WIDTH - 1)], axis=0)
    prev_tail = jnp.concatenate(
        [jnp.broadcast_to(tail[g:g + 1, :], (SUBLANES, LRU_WIDTH)) for g in range(CONV_WIDTH - 1)], axis=0)
    sub = lax.broadcasted_iota(jnp.int32, (halo, LRU_WIDTH), 0) % SUBLANES
    xbuf[0:halo, :] = jnp.where(sub == 0, prev_tail, shifted)
    for g in range(CONV_WIDTH - 1):
        tail[g:g + 1, :] = last[g * SUBLANES + nchunk - 1:g * SUBLANES + nchunk, :]
    xc = cb_ref[...]
    for k in range(CONV_WIDTH):
        lo = k * SUBLANES
        xc = xc + xbuf[lo:lo + ts, :] * cw_ref[k:k + 1, :]

    lam = lam_ref[...]
    neg_sp = -(jnp.maximum(-lam, 0.0) + jnp.log1p(jnp.exp(-jnp.abs(lam))))
    for n in range(LRU_BLOCKS):
        sl = slice(n * LRU_BLOCK_W, (n + 1) * LRU_BLOCK_W)
        xb = xc[:, sl]
        xb16 = xb.astype(BF16)
        gx = jax.nn.sigmoid(jnp.dot(xb16, wgx_ref[n], preferred_element_type=F32) + bgx_ref[n])
        ga = jax.nn.sigmoid(jnp.dot(xb16, wga_ref[n], preferred_element_type=F32) + bga_ref[n])
        log_a = LRU_C * ga * neg_sp[:, sl]
        a = jnp.exp(log_a)
        one_m_a2 = -jnp.tanh(log_a) * (1.0 + a * a)
        mult = jnp.where(one_m_a2 > 0.0, one_m_a2 * lax.rsqrt(one_m_a2), 0.0)
        a_scr[:, sl] = a
        b_scr[:, sl] = mult * (gx * xb)

    h_end = jnp.zeros((SUBLANES, LRU_WIDTH), F32)
    a_end = jnp.ones((SUBLANES, LRU_WIDTH), F32)
    for g in range(clen):
        r = g * SUBLANES
        a8 = a_scr[r:r + SUBLANES, :]
        h_end = a8 * h_end + b_scr[r:r + SUBLANES, :]
        a_end = a8 * a_end
        b_scr[r:r + SUBLANES, :] = h_end
        a_scr[r:r + SUBLANES, :] = a_end
    state = hcar[...]
    starts = []
    for j in range(nchunk):
        starts.append(state)
        state = h_end[j:j + 1, :] + a_end[j:j + 1, :] * state
    hcar[...] = state
    start = jnp.concatenate(starts, axis=0)
    hl = b_scr[...].reshape(clen, SUBLANES, LRU_WIDTH)
    ap = a_scr[...].reshape(clen, SUBLANES, LRU_WIDTH)
    hfull = (hl + ap * start[None]).reshape(ts, LRU_WIDTH)
    y = (hfull * (gate * jax.nn.sigmoid(gate))).astype(BF16)
    y_ref[0] = jnp.dot(unperm_ref[...], y, preferred_element_type=F32).astype(BF16)


def _lru_call(x, ng, w, cw, cb, wgx, bgx, wga, bga, lam, ts):
    B, S, _ = x.shape
    clen = ts // SUBLANES
    rho = np.arange(ts)
    step_of_row = (rho % SUBLANES) * clen + rho // SUBLANES
    perm_np = (step_of_row[:, None] == np.arange(ts)[None, :]).astype(np.float32)
    perm = jnp.asarray(perm_np, BF16)
    unperm = jnp.asarray(perm_np.T, BF16)
    return pl.pallas_call(
        functools.partial(_lru_kernel, ts=ts),
        grid=(B, S // ts),
        in_specs=[pl.BlockSpec((1, ts, D_MODEL), lambda b, t: (b, t, 0)),
                  _const_spec(ng.shape), _const_spec(w.shape), _const_spec(cw.shape),
                  _const_spec(cb.shape), _const_spec(wgx.shape), _const_spec(bgx.shape),
                  _const_spec(wga.shape), _const_spec(bga.shape), _const_spec(lam.shape),
                  _const_spec(perm.shape), _const_spec(unperm.shape)],
        out_specs=pl.BlockSpec((1, ts, LRU_WIDTH), lambda b, t: (b, t, 0)),
        out_shape=jax.ShapeDtypeStruct((B, S, LRU_WIDTH), BF16),
        scratch_shapes=[pltpu.VMEM((ts + (CONV_WIDTH - 1) * SUBLANES, LRU_WIDTH), F32),
                        pltpu.VMEM((ts, LRU_WIDTH), F32),
                        pltpu.VMEM((ts, LRU_WIDTH), F32),
                        pltpu.VMEM((SUBLANES, LRU_WIDTH), F32),
                        pltpu.VMEM((1, LRU_WIDTH), F32)],
        compiler_params=pltpu.CompilerParams(
            dimension_semantics=("parallel", "arbitrary"), vmem_limit_bytes=VMEM_LIMIT),
        name="lru_branch",
    )(x, ng, w, cw, cb, wgx, bgx, wga, bga, lam, perm, unperm)


def _mla_prep_kernel(x_ref, ng_ref, wb_ref, gcq_ref, gckv_ref, wuq_ref, wuk_ref, wuv_ref,
                     gq_ref, gk_ref, ones_ref, vone_ref, cos_ref, sin_ref, q_ref, k_ref, v_ref):
    h = _rms(x_ref[0], ng_ref[...]).astype(BF16)
    z = jnp.dot(h, wb_ref[...], preferred_element_type=F32)
    cq = _rms(z[:, :Q_LORA], gcq_ref[...]).astype(BF16)
    ckv = _rms(z[:, Q_LORA:Q_LORA + KV_LORA], gckv_ref[...]).astype(BF16)
    kr = z[:, Q_LORA + KV_LORA:]
    cos = cos_ref[0]
    sin = sin_ref[0]
    q = jnp.dot(cq, wuq_ref[...], preferred_element_type=F32)
    q = _head_rms_rope(q, ones_ref, MLA_QK, gq_ref[...], cos, sin) * (MLA_QK ** -0.5 * LOG2E)
    k = jnp.dot(ckv, wuk_ref[...], preferred_element_type=F32)
    k = k + jnp.concatenate([kr] * MLA_HEADS, axis=-1)
    k = _head_rms_rope(k, ones_ref, MLA_QK, gk_ref[...], cos, sin)
    for hh in range(MLA_HEADS):
        q_ref[0, hh] = q[:, hh * LANES:(hh + 1) * LANES].astype(BF16)
        k_ref[0, hh] = k[:, hh * LANES:(hh + 1) * LANES].astype(BF16)
    v = jnp.dot(ckv, wuv_ref[...], preferred_element_type=F32) + vone_ref[...]
    for hh in range(MLA_HEADS):
        v_ref[0, hh] = v[:, hh * LANES:(hh + 1) * LANES].astype(BF16)


def _mla_prep_call(x, ng, wb, gcq, gckv, wuq, wuk, wuv, gq, gk, ones, vone, cos_t, sin_t, tm):
    B, S, _ = x.shape
    tab = pl.BlockSpec((1, tm, LANES), lambda b, i: (b, i, 0))
    consts = (ng, wb, gcq, gckv, wuq, wuk, wuv, gq, gk, ones, vone)
    return pl.pallas_call(
        _mla_prep_kernel,
        grid=(B, S // tm),
        in_specs=[pl.BlockSpec((1, tm, D_MODEL), lambda b, i: (b, i, 0))]
        + [_const_spec(a.shape) for a in consts] + [tab, tab],
        out_specs=[pl.BlockSpec((1, MLA_HEADS, tm, LANES), lambda b, i: (b, 0, i, 0))] * 3,
        out_shape=[jax.ShapeDtypeStruct((B, MLA_HEADS, S, LANES), BF16)] * 3,
        compiler_params=pltpu.CompilerParams(
            dimension_semantics=("parallel", "parallel"), vmem_limit_bytes=VMEM_LIMIT),
        name="mla_prep",
    )(x, *consts, cos_t, sin_t)


def _mla_attn_kernel(q_ref, k_ref, v_ref, o_ref, *, tq, tk):
    qi = pl.program_id(2)
    row = lax.broadcasted_iota(jnp.int32, (tk, tk), 0)
    col = lax.broadcasted_iota(jnp.int32, (tk, tk), 1)
    causal = col <= row

    def update(state, s, v):
        m, acc = state
        m_new = jnp.maximum(m, jnp.max(s, axis=-1, keepdims=True))
        p = jnp.exp2((s - m_new).astype(BF16))
        return m_new, jnp.exp2(m - m_new) * acc + jnp.dot(p, v, preferred_element_type=F32)

    def kv(hh, j):
        start = pl.multiple_of(j * tk, tk)
        return k_ref[0, hh, pl.ds(start, tk), :], v_ref[0, hh, pl.ds(start, tk), :]

    def scores(qv, k):
        return lax.dot_general(qv, k, (((1,), (1,)), ((), ())), preferred_element_type=F32)

    def step(j, carry):
        new = []
        for hh in range(2):
            k, v = kv(hh, j)
            s = scores(q_ref[0, hh], k)
            new.append((update(carry[hh][0], s[:tk], v), update(carry[hh][1], s[tk:], v)))
        return tuple(new)

    def init_half():
        return jnp.full((tk, 1), NEG_INF, F32), jnp.zeros((tk, LANES), F32)

    carry = lax.fori_loop(0, qi, lambda t, c: step(2 * t + 1, step(2 * t, c)),
                          tuple((init_half(), init_half()) for _ in range(2)))
    outs = []
    for hh in range(2):
        k, v = kv(hh, 2 * qi)
        s = scores(q_ref[0, hh], k)
        _, acc_a = update(carry[hh][0], jnp.where(causal, s[:tk], NEG_INF), v)
        half_b = update(carry[hh][1], s[tk:], v)
        k, v = kv(hh, 2 * qi + 1)
        s = scores(q_ref[0, hh, tk:, :], k)
        _, acc_b = update(half_b, jnp.where(causal, s, NEG_INF), v)
        outs.append(jnp.concatenate([acc_a, acc_b], axis=0))
    lane = lax.broadcasted_iota(jnp.int32, (tq, LANES), 1)
    o = jnp.where(lane < MLA_V, outs[0] / pltpu.roll(outs[0], MLA_V, 1),
                  outs[1] / pltpu.roll(outs[1], MLA_V, 1))
    o_ref[0] = o.astype(BF16)


def _mla_attn_call(q, k, v, tk):
    B, H, S, _ = q.shape
    tq = 2 * tk
    return pl.pallas_call(
        functools.partial(_mla_attn_kernel, tq=tq, tk=tk),
        grid=(B, H // 2, S // tq),
        in_specs=[pl.BlockSpec((1, 2, tq, LANES), lambda b, p, i: (b, p, i, 0)),
                  pl.BlockSpec((1, 2, S, LANES), lambda b, p, i: (b, p, 0, 0)),
                  pl.BlockSpec((1, 2, S, LANES), lambda b, p, i: (b, p, 0, 0))],
        out_specs=pl.BlockSpec((1, tq, LANES), lambda b, p, i: (b, i, p)),
        out_shape=jax.ShapeDtypeStruct((B, S, MLA_WIDTH), BF16),
        compiler_params=pltpu.CompilerParams(
            dimension_semantics=("parallel", "parallel", "arbitrary"),
            vmem_limit_bytes=VMEM_LIMIT),
        name="mla_attn",
    )(q, k, v)


def _residue_major(ref_at, dil, n):
    if dil == 1:
        return ref_at[...]
    return jnp.concatenate([ref_at[pl.ds(r, n, stride=dil), :] for r in range(dil)], axis=0)


def _dil_prep_kernel(x_ref, ng_ref, w_ref, g_ref, ones_ref, cos_ref, sin_ref,
                     o0_ref, o1_ref, o2_ref, xs_scr, *, tm):
    xn = _rms(x_ref[0], ng_ref[...])
    nslab = D_MODEL // LANES
    for c in range(nslab):
        xs_scr[c] = xn[:, c * LANES:(c + 1) * LANES]
    outs = (o0_ref, o1_ref, o2_ref)
    for g, (_, dil) in enumerate(DIL_GROUPS):
        n = tm // dil
        h = jnp.concatenate([_residue_major(xs_scr.at[c], dil, n) for c in range(nslab)],
                            axis=-1).astype(BF16)
        cos = _residue_major(cos_ref.at[0], dil, n)
        sin = _residue_major(sin_ref.at[0], dil, n)
        for j in range(3):
            z = jnp.dot(h, w_ref[3 * g + j], preferred_element_type=F32)
            if j < 2:
                z = _head_rms_rope(z, ones_ref, DIL_HD, g_ref[3 * g + j], cos, sin)
            if j == 0:
                z = z * (DIL_HD ** -0.5 * LOG2E)
            z = z.astype(BF16)
            for r in range(dil):
                outs[g][j, 0, r] = z[r * n:(r + 1) * n]


def _dil_prep_call(x, ng, wd, gd, ones, cos_t, sin_t, tm):
    B, S, _ = x.shape
    tab = pl.BlockSpec((1, tm, LANES), lambda b, i: (b, i, 0))
    out_specs, out_shape = [], []
    for _, dil in DIL_GROUPS:
        out_specs.append(pl.BlockSpec((3, 1, dil, tm // dil, DIL_GW), lambda b, i: (0, b, 0, i, 0)))
        out_shape.append(jax.ShapeDtypeStruct((3, B, dil, S // dil, DIL_GW), BF16))
    return pl.pallas_call(
        functools.partial(_dil_prep_kernel, tm=tm),
        grid=(B, S // tm),
        in_specs=[pl.BlockSpec((1, tm, D_MODEL), lambda b, i: (b, i, 0)),
                  _const_spec(ng.shape), _const_spec(wd.shape), _const_spec(gd.shape),
                  _const_spec(ones.shape), tab, tab],
        out_specs=out_specs,
        out_shape=out_shape,
        scratch_shapes=[pltpu.VMEM((D_MODEL // LANES, tm, LANES), F32)],
        compiler_params=pltpu.CompilerParams(
            dimension_semantics=("parallel", "parallel"), vmem_limit_bytes=VMEM_LIMIT),
        name="dil_prep",
    )(x, ng, wd, gd, ones, cos_t, sin_t)


def _dil_attn_kernel(q_ref, kc_ref, kp_ref, vc_ref, vp_ref, o_ref, lse_ref, kbuf, vbuf, *, rows):
    i = pl.program_id(2)
    nk = DIL_NK
    kbuf[0:nk, :] = kp_ref[0, 0, 0]
    kbuf[nk:nk + rows, :] = kc_ref[0, 0, 0]
    vbuf[0:nk, :] = vp_ref[0, 0, 0]
    vbuf[nk:nk + rows, :] = vc_ref[0, 0, 0]

    qi = lax.broadcasted_iota(jnp.int32, (2 * nk, 2 * nk), 0) % nk
    ki = lax.broadcasted_iota(jnp.int32, (2 * nk, 2 * nk), 1)
    band = (ki >= qi) & (ki <= qi + nk)
    band_first = band & (ki >= jnp.where(i > 0, 0, nk))
    lane = lax.broadcasted_iota(jnp.int32, (nk, LANES), 1)
    head_a = (lane // (DIL_HD // 2)) % 2 == 0

    for n in range(rows // nk):
        mask = band_first if n == 0 else band
        lse_acc = jnp.zeros((nk, LANES), F32)
        for p in range(DIL_HEADS // 2):
            cs = slice(p * LANES, (p + 1) * LANES)
            qp = q_ref[0, 0, 0, n * nk:(n + 1) * nk, cs]
            zero = jnp.zeros_like(qp)
            q2 = jnp.concatenate([jnp.where(head_a, qp, zero), jnp.where(head_a, zero, qp)], axis=0)
            kk = kbuf[n * nk:(n + 2) * nk, cs]
            vv = vbuf[n * nk:(n + 2) * nk, cs]
            s = lax.dot_general(q2, kk, (((1,), (1,)), ((), ())), preferred_element_type=F32)
            s = jnp.where(mask, s, NEG_INF)
            m = jnp.max(s, axis=-1, keepdims=True)
            e = jnp.exp2(s - m)
            den = jnp.sum(e, axis=-1, keepdims=True)
            pv = jnp.dot(e.astype(BF16), vv, preferred_element_type=F32) / den
            lse = (m + jnp.log2(den)) * LN2
            o_ref[0, 0, n * nk:(n + 1) * nk, cs] = jnp.where(
                lane < DIL_HD, pv[:nk], pv[nk:]).astype(o_ref.dtype)
            lse_acc = jnp.where(lane == 2 * p, lse[:nk], lse_acc)
            lse_acc = jnp.where(lane == 2 * p + 1, lse[nk:], lse_acc)
        lse_ref[0, 0, n * nk:(n + 1) * nk, :] = lse_acc


def _dil_attn_call(qkv, g, rows):
    _, B, dil, M, _ = qkv.shape
    nb = rows // DIL_NK
    cur = lambda ch: pl.BlockSpec((1, 1, 1, rows, DIL_GW), lambda b, r, i: (ch, b, r, i, 0))
    prev = lambda ch: pl.BlockSpec((1, 1, 1, DIL_NK, DIL_GW),
                                   lambda b, r, i: (ch, b, r, jnp.maximum(i * nb - 1, 0), 0))
    return pl.pallas_call(
        functools.partial(_dil_attn_kernel, rows=rows),
        grid=(B, dil, M // rows),
        in_specs=[cur(0), cur(1), prev(1), cur(2), prev(2)],
        out_specs=[pl.BlockSpec((1, 1, rows, DIL_GW), lambda b, r, i: (b, r, i, 0)),
                   pl.BlockSpec((1, 1, rows, LANES), lambda b, r, i: (b, r, i, 0))],
        out_shape=[jax.ShapeDtypeStruct((B, dil, M, DIL_GW), BF16),
                   jax.ShapeDtypeStruct((B, dil, M, LANES), F32)],
        scratch_shapes=[pltpu.VMEM((rows + DIL_NK, DIL_GW), BF16),
                        pltpu.VMEM((rows + DIL_NK, DIL_GW), BF16)],
        compiler_params=pltpu.CompilerParams(
            dimension_semantics=("parallel", "parallel", "arbitrary"),
            vmem_limit_bytes=VMEM_LIMIT),
        name=f"dil_attn_g{g}",
    )(qkv, qkv, qkv, qkv, qkv)


def _token_major(ref, dil, scr):
    n = ref.shape[2]
    nslab = ref.shape[3] // LANES
    if dil == 1:
        return ref[0, 0].astype(F32)
    for r in range(dil):
        blk = ref[0, r].astype(F32)
        for c in range(nslab):
            scr[c, pl.ds(r, n, stride=dil), :] = blk[:, c * LANES:(c + 1) * LANES]
    return jnp.concatenate([scr[c] for c in range(nslab)], axis=-1)


def _merge_kernel(x_ref, ylru_ref, omla_ref, od0_ref, od1_ref, od2_ref, l0_ref, l1_ref, l2_ref,
                  ng_ref, wf_ref, bm_ref, plru_ref, pmla_ref, pdil_ref, wout_ref, exp_ref, out_ref,
                  o_scr, l_scr):
    x = x_ref[0]
    h = _rms(x, ng_ref[...]).astype(BF16)

    def proj(lo, width):
        return jnp.dot(h, wf_ref[:, lo:lo + width], preferred_element_type=F32)

    g_off = N_BRANCH * D_MODEL
    mla_gate = proj(g_off, MLA_WIDTH)
    dil_gate = proj(g_off + MLA_WIDTH, DIL_WIDTH)
    y_mla = (omla_ref[0].astype(F32) * (mla_gate * jax.nn.sigmoid(mla_gate))).astype(BF16)

    od_refs = (od0_ref, od1_ref, od2_ref)
    l_refs = (l0_ref, l1_ref, l2_ref)
    lses = [_token_major(l_refs[g], dil, l_scr.at[g]) for g, (_, dil) in enumerate(DIL_GROUPS)]
    mx = jnp.maximum(jnp.maximum(lses[0], lses[1]), lses[2])
    es = [jnp.exp(l - mx) for l in lses]
    den = es[0] + es[1] + es[2]
    o_dil = None
    for g, (_, dil) in enumerate(DIL_GROUPS):
        w = _split_dot(es[g] / den, exp_ref[...])
        term = w * _token_major(od_refs[g], dil, o_scr.at[g])
        o_dil = term if o_dil is None else o_dil + term
    y_dil = (o_dil * (dil_gate * jax.nn.sigmoid(dil_gate))).astype(BF16)

    ys = [ylru_ref[0], y_mla, y_dil]
    ps = [plru_ref, pmla_ref, pdil_ref]
    merged = None
    for br in range(N_BRANCH):
        gate = jax.nn.sigmoid(proj(br * D_MODEL, D_MODEL) + bm_ref[:, br * D_MODEL:(br + 1) * D_MODEL])
        term = gate * jnp.dot(ys[br], ps[br][...], preferred_element_type=F32)
        merged = term if merged is None else merged + term
    out_ref[0] = x + jnp.dot(merged.astype(BF16), wout_ref[...], preferred_element_type=F32)


def _merge_call(x, ylru, omla, ods, lses, ng, wf, bm, plru, pmla, pdil, wout, expand, tm):
    B, S, _ = x.shape
    tok = lambda w: pl.BlockSpec((1, tm, w), lambda b, i: (b, i, 0))
    res = lambda dil, w: pl.BlockSpec((1, dil, tm // dil, w), lambda b, i: (b, 0, i, 0))
    dils = [dil for _, dil in DIL_GROUPS]
    consts = (ng, wf, bm, plru, pmla, pdil, wout, expand)
    ngrp = len(DIL_GROUPS)
    return pl.pallas_call(
        _merge_kernel,
        grid=(B, S // tm),
        in_specs=[tok(D_MODEL), tok(LRU_WIDTH), tok(MLA_WIDTH)]
        + [res(d, DIL_GW) for d in dils] + [res(d, LANES) for d in dils]
        + [_const_spec(a.shape) for a in consts],
        out_specs=tok(D_MODEL),
        out_shape=jax.ShapeDtypeStruct((B, S, D_MODEL), F32),
        scratch_shapes=[pltpu.VMEM((ngrp, DIL_GW // LANES, tm, LANES), F32),
                        pltpu.VMEM((ngrp, 1, tm, LANES), F32)],
        compiler_params=pltpu.CompilerParams(
            dimension_semantics=("parallel", "parallel"), vmem_limit_bytes=VMEM_LIMIT),
        name="merge",
    )(x, ylru, omla, *ods, *lses, *consts)


def _mla_head_cols(w, head_stride, feat_lo, feat_hi):
    ok = (_MLA_FEAT >= feat_lo) & (_MLA_FEAT < feat_hi)
    idx = np.concatenate([h * head_stride + np.where(ok, _MLA_FEAT, 0) for h in range(MLA_HEADS)])
    keep = np.concatenate([ok] * MLA_HEADS)
    return jnp.where(keep[None, :], w[:, idx], 0.0)


def _mla_gain_lanes(g):
    ok = _MLA_FEAT >= 0
    lane = jnp.where(ok, g[np.where(ok, _MLA_FEAT, 0)], 0.0)
    return jnp.concatenate([lane] * MLA_HEADS)[None, :]


def kernel(x, positions, norm_g, w_in, conv_w, conv_b, w_gate_x, b_gate_x, w_gate_a, b_gate_a,
           lru_lambda, w_lru_o, cq_norm_g, ckv_norm_g, w_uq, w_ukv, mla_q_norm_g, mla_k_norm_g,
           w_mla_o, dil_q_norm_g, dil_k_norm_g, w_dil_o, b_merge, w_out):
    B, S, _ = x.shape
    tm = min(512, S)
    cos_d, sin_d, cos_m, sin_m = _rope_tables(positions, tm)
    dil_ones = jnp.asarray(_DIL_ONES, BF16)
    mla_ones = jnp.asarray(_MLA_ONES, BF16)
    v_one = jnp.asarray(((np.arange(MLA_HEADS * LANES) // MLA_V + 1) // 2 % 2)[None, :], F32)
    expand = jnp.asarray(
        (np.arange(LANES)[:, None] == np.arange(DIL_GW)[None, :] // DIL_HD).astype(np.float32), BF16)

    for l in range(DEPTH):
        wl = w_in[l]
        ng = norm_g[l][None, :]

        y_lru = _lru_call(
            x, ng, wl[:, OFF_LRU_X:OFF_CQ].astype(BF16), conv_w[l], conv_b[l][None, :],
            w_gate_x[l].astype(BF16), b_gate_x[l][:, None, :],
            w_gate_a[l].astype(BF16), b_gate_a[l][:, None, :], lru_lambda[l][None, :], tm)

        kr_ok = _MLA_FEAT >= MLA_NOPE
        w_kr = jnp.where(kr_ok[None, :],
                         wl[:, OFF_KR + np.where(kr_ok, _MLA_FEAT - MLA_NOPE, 0)], 0.0)
        wb = jnp.concatenate([wl[:, OFF_CQ:OFF_KR], w_kr], axis=1).astype(BF16)
        wuq = _mla_head_cols(w_uq[l], MLA_QK, 0, MLA_QK).astype(BF16)
        wuk = _mla_head_cols(w_ukv[l], MLA_NOPE + MLA_V, 0, MLA_NOPE).astype(BF16)
        wuv = w_ukv[l].reshape(KV_LORA, MLA_HEADS // 2, 2, MLA_NOPE + MLA_V)[..., MLA_NOPE:]
        zv = jnp.zeros_like(wuv[:, :, 0])
        wuv = jnp.stack([wuv[:, :, 0], zv, zv, wuv[:, :, 1]], axis=2)
        wuv = wuv.reshape(KV_LORA, MLA_HEADS * LANES).astype(BF16)
        q, k, v = _mla_prep_call(
            x, ng, wb, cq_norm_g[l][None, :], ckv_norm_g[l][None, :], wuq, wuk, wuv,
            _mla_gain_lanes(mla_q_norm_g[l]), _mla_gain_lanes(mla_k_norm_g[l]),
            mla_ones, v_one, cos_m, sin_m, tm)
        o_mla = _mla_attn_call(q, k, v, min(tm, S // 2))

        chunks, gains = [], []
        gq = jnp.concatenate([dil_q_norm_g[l][_DIL_FEAT]] * (DIL_GW // LANES))
        gk = jnp.concatenate([dil_k_norm_g[l][_DIL_FEAT]] * (DIL_GW // LANES))
        for g in range(len(DIL_GROUPS)):
            chunks += [wl[:, OFF_DQ + g * DIL_GW + _DIL_PERM],
                       wl[:, OFF_DK + g * DIL_GW + _DIL_PERM],
                       wl[:, OFF_DV + g * DIL_GW:OFF_DV + (g + 1) * DIL_GW]]
            gains += [gq, gk, jnp.ones_like(gq)]
        wd = jnp.stack(chunks).astype(BF16)
        gd = jnp.stack(gains)[:, None, :]
        qkvs = _dil_prep_call(x, ng, wd, gd, dil_ones, cos_d, sin_d, tm)
        ods, lses = [], []
        for g, (window, dil) in enumerate(DIL_GROUPS):
            assert window // dil == DIL_NK
            o_g, lse_g = _dil_attn_call(qkvs[g], g, min(2 * DIL_NK, S // dil))
            ods.append(o_g)
            lses.append(lse_g)

        wf = jnp.concatenate([wl[:, OFF_MERGE:], wl[:, OFF_MLA_G:OFF_DQ], wl[:, OFF_DIL_G:OFF_MERGE]],
                             axis=1).astype(BF16)
        x = _merge_call(x, y_lru, o_mla, ods, lses, ng, wf, b_merge[l][None, :],
                        w_lru_o[l].astype(BF16), w_mla_o[l].astype(BF16), w_dil_o[l].astype(BF16),
                        w_out[l].astype(BF16), expand, tm)
    return x
```

```python
import functools
import math

import numpy as np
import jax
import jax.numpy as jnp
from jax import lax
from jax.experimental import pallas as pl
from jax.experimental.pallas import tpu as pltpu

F32 = jnp.float32
BF16 = jnp.bfloat16

D_MODEL = 1024
DEPTH = 2
EPS = 1e-6
ROPE_THETA = 10000.0

LRU_WIDTH = 1024
LRU_BLOCKS = 8
LRU_BLOCK_W = LRU_WIDTH // LRU_BLOCKS
CONV_WIDTH = 4
LRU_C = 8.0

MLA_HEADS = 8
MLA_NOPE = 64
MLA_ROPE = 32
MLA_QK = MLA_NOPE + MLA_ROPE
MLA_V = 64
Q_LORA = 256
KV_LORA = 128
MLA_WIDTH = MLA_HEADS * MLA_V

DIL_GROUPS = ((128, 1), (512, 4), (2048, 16))
DIL_HEADS = 8
DIL_HD = 64
DIL_GW = DIL_HEADS * DIL_HD
DIL_QKV = len(DIL_GROUPS) * DIL_GW
DIL_WIDTH = DIL_GW
DIL_NK = 128

N_BRANCH = 3
OFF_LRU_X = 0
OFF_LRU_G = OFF_LRU_X + LRU_WIDTH
OFF_CQ = OFF_LRU_G + LRU_WIDTH
OFF_CKV = OFF_CQ + Q_LORA
OFF_KR = OFF_CKV + KV_LORA
OFF_MLA_G = OFF_KR + MLA_ROPE
OFF_DQ = OFF_MLA_G + MLA_WIDTH
OFF_DK = OFF_DQ + DIL_QKV
OFF_DV = OFF_DK + DIL_QKV
OFF_DIL_G = OFF_DV + DIL_QKV
OFF_MERGE = OFF_DIL_G + DIL_WIDTH

LANES = 128
SUBLANES = 8
VMEM_LIMIT = 48 * 1024 * 1024

NEG_INF = float("-inf")
LOG2E = math.log2(math.e)
LN2 = math.log(2.0)


def _mla_lane_feat():
    feat = -np.ones((LANES,), np.int64)
    feat[0:48] = np.arange(48)
    feat[48:64] = MLA_NOPE + np.arange(16)
    feat[64:80] = 48 + np.arange(16)
    feat[112:128] = MLA_NOPE + 16 + np.arange(16)
    return feat


_MLA_FEAT = _mla_lane_feat()


def _block_ones(width, head_of_lane):
    m = (head_of_lane[:, None] == head_of_lane[None, :]).astype(np.float32)
    assert m.shape == (width, width)
    return m


_DIL_HEAD_OF_LANE = (np.arange(2 * LANES) // LANES) * 2 + (np.arange(2 * LANES) % LANES // 32) % 2
_DIL_ONES = _block_ones(2 * LANES, _DIL_HEAD_OF_LANE)
_MLA_ONES = _block_ones(2 * LANES, np.arange(2 * LANES) // LANES)


def _const_spec(shape):
    nd = len(shape)
    return pl.BlockSpec(shape, lambda *_: (0,) * nd, pipeline_mode=pl.Buffered(1))


def _rms(xf, g_row):
    y = xf * lax.rsqrt(jnp.mean(xf * xf, axis=-1, keepdims=True) + EPS)
    return y * g_row


def _head_rms_rope(z, ones_ref, head_dim, g_row, cos, sin):
    n = z.shape[-1]
    parts = []
    for c in range(n // (2 * LANES)):
        zc = z[:, c * 2 * LANES:(c + 1) * 2 * LANES]
        ss = jnp.dot((zc * zc).astype(BF16), ones_ref[...], preferred_element_type=F32)
        y = zc * lax.rsqrt(ss * (1.0 / head_dim) + EPS) * g_row[:, c * 2 * LANES:(c + 1) * 2 * LANES]
        for s in range(2):
            ys = y[:, s * LANES:(s + 1) * LANES]
            parts.append(ys * cos + pltpu.roll(ys, LANES // 2, 1) * sin)
    return jnp.concatenate(parts, axis=-1)


def _rope_kernel(pos_ref, inv_ref, sgn_ref, cd_ref, sd_ref, cm_ref, sm_ref):
    pos = pos_ref[0].astype(F32)
    for t, (c_ref, s_ref) in enumerate(((cd_ref, sd_ref), (cm_ref, sm_ref))):
        ang = pos * inv_ref[t:t + 1, :]
        c_ref[0] = jnp.cos(ang)
        s_ref[0] = jnp.sin(ang) * sgn_ref[t:t + 1, :]


def _rope_tables(positions, tm):
    B, S = positions.shape
    inv_d = ROPE_THETA ** (-jnp.arange(0, DIL_HD, 2, dtype=F32) / DIL_HD)
    inv_m = ROPE_THETA ** (-jnp.arange(0, MLA_ROPE, 2, dtype=F32) / MLA_ROPE)
    lane = np.arange(LANES)
    inv_d_l = inv_d[lane % (DIL_HD // 2)]
    sgn_d_l = np.where(lane < LANES // 2, -1.0, 1.0)
    rope_feat = _MLA_FEAT - MLA_NOPE
    is_rope = rope_feat >= 0
    inv_m_l = jnp.where(is_rope, inv_m[np.where(is_rope, rope_feat % (MLA_ROPE // 2), 0)], 0.0)
    sgn_m_l = np.where(is_rope, np.where(rope_feat < MLA_ROPE // 2, -1.0, 1.0), 0.0)
    inv = jnp.stack([inv_d_l, inv_m_l]).astype(F32)
    sgn = jnp.asarray(np.stack([sgn_d_l, sgn_m_l]), F32)
    out = jax.ShapeDtypeStruct((B, S, LANES), F32)
    tab = pl.BlockSpec((1, tm, LANES), lambda b, i: (b, i, 0))
    return pl.pallas_call(
        _rope_kernel,
        grid=(B, S // tm),
        in_specs=[pl.BlockSpec((1, tm, 1), lambda b, i: (b, i, 0)),
                  _const_spec((2, LANES)), _const_spec((2, LANES))],
        out_specs=[tab, tab, tab, tab],
        out_shape=[out, out, out, out],
        compiler_params=pltpu.CompilerParams(
            dimension_semantics=("parallel", "parallel"), vmem_limit_bytes=VMEM_LIMIT),
        name="rope_tables",
    )(positions.reshape(B, S, 1), inv, sgn)


def _lru_kernel(x_ref, ng_ref, w_ref, cw_ref, cb_ref, wgx_ref, bgx_ref, wga_ref, bga_ref,
                lam_ref, perm_ref, unperm_ref, y_ref, xbuf, a_scr, b_scr, tail, hcar, *, ts):
    t = pl.program_id(1)
    nchunk = SUBLANES
    clen = ts // nchunk
    halo = (CONV_WIDTH - 1) * SUBLANES

    @pl.when(t == 0)
    def _():
        tail[...] = jnp.zeros(tail.shape, F32)
        hcar[...] = jnp.zeros((1, LRU_WIDTH), F32)

    h = _rms(x_ref[0], ng_ref[...]).astype(BF16)
    h = jnp.dot(perm_ref[...], h, preferred_element_type=F32).astype(BF16)
    z = jnp.dot(h, w_ref[...], preferred_element_type=F32)
    lx = z[:, :LRU_WIDTH]
    gate = z[:, LRU_WIDTH:]
    xbuf[halo:halo + ts, :] = lx
    last = lx[ts - halo:, :]
    shifted = jnp.concatenate(
        [pltpu.roll(last[g * SUBLANES:(g + 1) * SUBLANES], 1, 0) for g in range(CONV_WIDTH - 1)], axis=0)
    prev_tail = jnp.concatenate(
        [jnp.broadcast_to(tail[g:g + 1, :], (SUBLANES, LRU_WIDTH)) for g in range(CONV_WIDTH - 1)], axis=0)
    sub = lax.broadcasted_iota(jnp.int32, (halo, LRU_WIDTH), 0) % SUBLANES
    xbuf[0:halo, :] = jnp.where(sub == 0, prev_tail, shifted)
    for g in range(CONV_WIDTH - 1):
        tail[g:g + 1, :] = last[g * SUBLANES + nchunk - 1:g * SUBLANES + nchunk, :]
    xc = cb_ref[...]
    for k in range(CONV_WIDTH):
        lo = k * SUBLANES
        xc = xc + xbuf[lo:lo + ts, :] * cw_ref[k:k + 1, :]

    lam = lam_ref[...]
    neg_sp = -(jnp.maximum(-lam, 0.0) + jnp.log1p(jnp.exp(-jnp.abs(lam))))
    for n in range(LRU_BLOCKS):
        sl = slice(n * LRU_BLOCK_W, (n + 1) * LRU_BLOCK_W)
        xb = xc[:, sl]
        xb16 = xb.astype(BF16)
        gx = jax.nn.sigmoid(jnp.dot(xb16, wgx_ref[n], preferred_element_type=F32) + bgx_ref[n])
        ga = jax.nn.sigmoid(jnp.dot(xb16, wga_ref[n], preferred_element_type=F32) + bga_ref[n])
        log_a = LRU_C * ga * neg_sp[:, sl]
        a = jnp.exp(log_a)
        one_m_a2 = -jnp.tanh(log_a) * (1.0 + a * a)
        mult = jnp.where(one_m_a2 > 0.0, one_m_a2 * lax.rsqrt(one_m_a2), 0.0)
        a_scr[:, sl] = a
        b_scr[:, sl] = mult * (gx * xb)

    h_end = jnp.zeros((SUBLANES, LRU_WIDTH), F32)
    a_end = jnp.ones((SUBLANES, LRU_WIDTH), F32)
    for g in range(clen):
        r = g * SUBLANES
        a8 = a_scr[r:r + SUBLANES, :]
        h_end = a8 * h_end + b_scr[r:r + SUBLANES, :]
        a_end = a8 * a_end
        b_scr[r:r + SUBLANES, :] = h_end
        a_scr[r:r + SUBLANES, :] = a_end
    state = hcar[...]
    starts = []
    for j in range(nchunk):
        starts.append(state)
        state = h_end[j:j + 1, :] + a_end[j:j + 1, :] * state
    hcar[...] = state
    start = jnp.concatenate(starts, axis=0)
    hl = b_scr[...].reshape(clen, SUBLANES, LRU_WIDTH)
    ap = a_scr[...].reshape(clen, SUBLANES, LRU_WIDTH)
    hfull = (hl + ap * start[None]).reshape(ts, LRU_WIDTH)
    y = (hfull * (gate * jax.nn.sigmoid(gate))).astype(BF16)
    y_ref[0] = jnp.dot(unperm_ref[...], y, preferred_element_type=F32).astype(BF16)


def _lru_call(x, ng, w, cw, cb, wgx, bgx, wga, bga, lam, ts):
    B, S, _ = x.shape
    clen = ts // SUBLANES
    rho = np.arange(ts)
    step_of_row = (rho % SUBLANES) * clen + rho // SUBLANES
    perm_np = (step_of_row[:, None] == np.arange(ts)[None, :]).astype(np.float32)
    perm = jnp.asarray(perm_np, BF16)
    unperm = jnp.asarray(perm_np.T, BF16)
    return pl.pallas_call(
        functools.partial(_lru_kernel, ts=ts),
        grid=(B, S // ts),
        in_specs=[pl.BlockSpec((1, ts, D_MODEL), lambda b, t: (b, t, 0)),
                  _const_spec(ng.shape), _const_spec(w.shape), _const_spec(cw.shape),
                  _const_spec(cb.shape), _const_spec(wgx.shape), _const_spec(bgx.shape),
                  _const_spec(wga.shape), _const_spec(bga.shape), _const_spec(lam.shape),
                  _const_spec(perm.shape), _const_spec(unperm.shape)],
        out_specs=pl.BlockSpec((1, ts, LRU_WIDTH), lambda b, t: (b, t, 0)),
        out_shape=jax.ShapeDtypeStruct((B, S, LRU_WIDTH), BF16),
        scratch_shapes=[pltpu.VMEM((ts + (CONV_WIDTH - 1) * SUBLANES, LRU_WIDTH), F32),
                        pltpu.VMEM((ts, LRU_WIDTH), F32),
                        pltpu.VMEM((ts, LRU_WIDTH), F32),
                        pltpu.VMEM((SUBLANES, LRU_WIDTH), F32),
                        pltpu.VMEM((1, LRU_WIDTH), F32)],
        compiler_params=pltpu.CompilerParams(
            dimension_semantics=("parallel", "arbitrary"), vmem_limit_bytes=VMEM_LIMIT),
        name="lru_branch",
    )(x, ng, w, cw, cb, wgx, bgx, wga, bga, lam, perm, unperm)


def _mla_prep_kernel(x_ref, ng_ref, wb_ref, gcq_ref, gckv_ref, wuq_ref, wuk_ref, wuv_ref,
                     gq_ref, gk_ref, ones_ref, vone_ref, cos_ref, sin_ref, q_ref, k_ref, v_ref):
    h = _rms(x_ref[0], ng_ref[...]).astype(BF16)
    z = jnp.dot(h, wb_ref[...], preferred_element_type=F32)
    cq = _rms(z[:, :Q_LORA], gcq_ref[...]).astype(BF16)
    ckv = _rms(z[:, Q_LORA:Q_LORA + KV_LORA], gckv_ref[...]).astype(BF16)
    kr = z[:, Q_LORA + KV_LORA:]
    cos = cos_ref[0]
    sin = sin_ref[0]
    q = jnp.dot(cq, wuq_ref[...], preferred_element_type=F32)
    q = _head_rms_rope(q, ones_ref, MLA_QK, gq_ref[...], cos, sin) * (MLA_QK ** -0.5 * LOG2E)
    k = jnp.dot(ckv, wuk_ref[...], preferred_element_type=F32)
    k = k + jnp.concatenate([kr] * MLA_HEADS, axis=-1)
    k = _head_rms_rope(k, ones_ref, MLA_QK, gk_ref[...], cos, sin)
    for hh in range(MLA_HEADS):
        q_ref[0, hh] = q[:, hh * LANES:(hh + 1) * LANES].astype(BF16)
        k_ref[0, hh] = k[:, hh * LANES:(hh + 1) * LANES].astype(BF16)
    v = jnp.dot(ckv, wuv_ref[...], preferred_element_type=F32) + vone_ref[...]
    for hh in range(MLA_HEADS):
        v_ref[0, hh] = v[:, hh * LANES:(hh + 1) * LANES].astype(BF16)


def _mla_prep_call(x, ng, wb, gcq, gckv, wuq, wuk, wuv, gq, gk, ones, vone, cos_t, sin_t, tm):
    B, S, _ = x.shape
    tab = pl.BlockSpec((1, tm, LANES), lambda b, i: (b, i, 0))
    consts = (ng, wb, gcq, gckv, wuq, wuk, wuv, gq, gk, ones, vone)
    return pl.pallas_call(
        _mla_prep_kernel,
        grid=(B, S // tm),
        in_specs=[pl.BlockSpec((1, tm, D_MODEL), lambda b, i: (b, i, 0))]
        + [_const_spec(a.shape) for a in consts] + [tab, tab],
        out_specs=[pl.BlockSpec((1, MLA_HEADS, tm, LANES), lambda b, i: (b, 0, i, 0))] * 3,
        out_shape=[jax.ShapeDtypeStruct((B, MLA_HEADS, S, LANES), BF16)] * 3,
        compiler_params=pltpu.CompilerParams(
            dimension_semantics=("parallel", "parallel"), vmem_limit_bytes=VMEM_LIMIT),
        name="mla_prep",
    )(x, *consts, cos_t, sin_t)


def _mla_attn_kernel(q_ref, k_ref, v_ref, o_ref, *, tq, tk):
    qi = pl.program_id(2)
    row = lax.broadcasted_iota(jnp.int32, (tk, tk), 0)
    col = lax.broadcasted_iota(jnp.int32, (tk, tk), 1)
    causal = col <= row

    def update(state, s, v):
        m, acc = state
        m_new = jnp.maximum(m, jnp.max(s, axis=-1, keepdims=True))
        p = jnp.exp2((s - m_new).astype(BF16))
        return m_new, jnp.exp2(m - m_new) * acc + jnp.dot(p, v, preferred_element_type=F32)

    def kv(hh, j):
        start = pl.multiple_of(j * tk, tk)
        return k_ref[0, hh, pl.ds(start, tk), :], v_ref[0, hh, pl.ds(start, tk), :]

    def scores(qv, k):
        return lax.dot_general(qv, k, (((1,), (1,)), ((), ())), preferred_element_type=F32)

    def step(j, carry):
        new = []
        for hh in range(2):
            k, v = kv(hh, j)
            s = scores(q_ref[0, hh], k)
            new.append((update(carry[hh][0], s[:tk], v), update(carry[hh][1], s[tk:], v)))
        return tuple(new)

    def init_half():
        return jnp.full((tk, 1), NEG_INF, F32), jnp.zeros((tk, LANES), F32)

    carry = lax.fori_loop(0, qi, lambda t, c: step(2 * t + 1, step(2 * t, c)),
                          tuple((init_half(), init_half()) for _ in range(2)))
    outs = []
    for hh in range(2):
        k, v = kv(hh, 2 * qi)
        s = scores(q_ref[0, hh], k)
        _, acc_a = update(carry[hh][0], jnp.where(causal, s[:tk], NEG_INF), v)
        half_b = update(carry[hh][1], s[tk:], v)
        k, v = kv(hh, 2 * qi + 1)
        s = scores(q_ref[0, hh, tk:, :], k)
        _, acc_b = update(half_b, jnp.where(causal, s, NEG_INF), v)
        outs.append(jnp.concatenate([acc_a, acc_b], axis=0))
    lane = lax.broadcasted_iota(jnp.int32, (tq, LANES), 1)
    o = jnp.where(lane < MLA_V, outs[0] / pltpu.roll(outs[0], MLA_V, 1),
                  outs[1] / pltpu.roll(outs[1], MLA_V, 1))
    o_ref[0] = o.astype(BF16)


def _mla_attn_call(q, k, v, tk):
    B, H, S, _ = q.shape
    tq = 2 * tk
    return pl.pallas_call(
        functools.partial(_mla_attn_kernel, tq=tq, tk=tk),
        grid=(B, H // 2, S // tq),
        in_specs=[pl.BlockSpec((1, 2, tq, LANES), lambda b, p, i: (b, p, i, 0)),
                  pl.BlockSpec((1, 2, S, LANES), lambda b, p, i: (b, p, 0, 0)),
                  pl.BlockSpec((1, 2, S, LANES), lambda b, p, i: (b, p, 0, 0))],
        out_specs=pl.BlockSpec((1, tq, LANES), lambda b, p, i: (b, i, p)),
        out_shape=jax.ShapeDtypeStruct((B, S, MLA_WIDTH), BF16),
        compiler_params=pltpu.CompilerParams(
            dimension_semantics=("parallel", "parallel", "arbitrary"),
            vmem_limit_bytes=VMEM_LIMIT),
        name="mla_attn",
    )(q, k, v)


def _residue_major(ref_at, dil, n):
    if dil == 1:
        return ref_at[...]
    return jnp.concatenate([ref_at[pl.ds(r, n, stride=dil), :] for r in range(dil)], axis=0)


def _dil_prep_kernel(x_ref, ng_ref, w_ref, g_ref, ones_ref, cos_ref, sin_ref,
                     o0_ref, o1_ref, o2_ref, xs_scr, *, tm):
    xn = _rms(x_ref[0], ng_ref[...])
    nslab = D_MODEL // LANES
    for c in range(nslab):
        xs_scr[c] = xn[:, c * LANES:(c + 1) * LANES]
    outs = (o0_ref, o1_ref, o2_ref)
    for g, (_, dil) in enumerate(DIL_GROUPS):
        n = tm // dil
        h = jnp.concatenate([_residue_major(xs_scr.at[c], dil, n) for c in range(nslab)],
                            axis=-1).astype(BF16)
        cos = _residue_major(cos_ref.at[0], dil, n)
        sin = _residue_major(sin_ref.at[0], dil, n)
        for j in range(3):
            z = jnp.dot(h, w_ref[3 * g + j], preferred_element_type=F32)
            if j < 2:
                z = _head_rms_rope(z, ones_ref, DIL_HD, g_ref[3 * g + j], cos, sin)
            if j == 0:
                z = z * (DIL_HD ** -0.5 * LOG2E)
            z = z.astype(BF16)
            for r in range(dil):
                outs[g][j, 0, r] = z[r * n:(r + 1) * n]


def _dil_prep_call(x, ng, wd, gd, ones, cos_t, sin_t, tm):
    B, S, _ = x.shape
    tab = pl.BlockSpec((1, tm, LANES), lambda b, i: (b, i, 0))
    out_specs, out_shape = [], []
    for _, dil in DIL_GROUPS:
        out_specs.append(pl.BlockSpec((3, 1, dil, tm // dil, DIL_GW), lambda b, i: (0, b, 0, i, 0)))
        out_shape.append(jax.ShapeDtypeStruct((3, B, dil, S // dil, DIL_GW), BF16))
    return pl.pallas_call(
        functools.partial(_dil_prep_kernel, tm=tm),
        grid=(B, S // tm),
        in_specs=[pl.BlockSpec((1, tm, D_MODEL), lambda b, i: (b, i, 0)),
                  _const_spec(ng.shape), _const_spec(wd.shape), _const_spec(gd.shape),
                  _const_spec(ones.shape), tab, tab],
        out_specs=out_specs,
        out_shape=out_shape,
        scratch_shapes=[pltpu.VMEM((D_MODEL // LANES, tm, LANES), F32)],
        compiler_params=pltpu.CompilerParams(
            dimension_semantics=("parallel", "parallel"), vmem_limit_bytes=VMEM_LIMIT),
        name="dil_prep",
    )(x, ng, wd, gd, ones, cos_t, sin_t)


def _dil_attn_kernel(q_ref, kc_ref, kp_ref, vc_ref, vp_ref, o_ref, lse_ref, kbuf, vbuf, *, rows):
    i = pl.program_id(2)
    nk = DIL_NK
    kbuf[0:nk, :] = kp_ref[0, 0, 0]
    kbuf[nk:nk + rows, :] = kc_ref[0, 0, 0]
    vbuf[0:nk, :] = vp_ref[0, 0, 0]
    vbuf[nk:nk + rows, :] = vc_ref[0, 0, 0]

    qi = lax.broadcasted_iota(jnp.int32, (2 * nk, 2 * nk), 0) % nk
    ki = lax.broadcasted_iota(jnp.int32, (2 * nk, 2 * nk), 1)
    band = (ki >= qi) & (ki <= qi + nk)
    band_first = band & (ki >= jnp.where(i > 0, 0, nk))
    lane = lax.broadcasted_iota(jnp.int32, (nk, LANES), 1)
    head_a = (lane // (DIL_HD // 2)) % 2 == 0

    for n in range(rows // nk):
        mask = band_first if n == 0 else band
        lse_acc = jnp.zeros((nk, LANES), F32)
        for p in range(DIL_HEADS // 2):
            cs = slice(p * LANES, (p + 1) * LANES)
            qp = q_ref[0, 0, 0, n * nk:(n + 1) * nk, cs]
            zero = jnp.zeros_like(qp)
            q2 = jnp.concatenate([jnp.where(head_a, qp, zero), jnp.where(head_a, zero, qp)], axis=0)
            kk = kbuf[n * nk:(n + 2) * nk, cs]
            vv = vbuf[n * nk:(n + 2) * nk, cs]
            s = lax.dot_general(q2, kk, (((1,), (1,)), ((), ())), preferred_element_type=F32)
            s = jnp.where(mask, s, NEG_INF)
            m = jnp.max(s, axis=-1, keepdims=True)
            e = jnp.exp2(s - m)
            den = jnp.sum(e, axis=-1, keepdims=True)
            pv = jnp.dot(e.astype(BF16), vv, preferred_element_type=F32) / den
            lse = (m + jnp.log2(den)) * LN2
            o_ref[0, 0, n * nk:(n + 1) * nk, cs] = jnp.where(
                lane < DIL_HD, pv[:nk], pv[nk:]).astype(o_ref.dtype)
            lse_acc = jnp.where(lane == 2 * p, lse[:nk], lse_acc)
            lse_acc = jnp.where(lane == 2 * p + 1, lse[nk:], lse_acc)
        lse_ref[0, 0, n * nk:(n + 1) * nk, :] = lse_acc


def _dil_attn_call(qkv, g, rows):
    _, B, dil, M, _ = qkv.shape
    nb = rows // DIL_NK
    cur = lambda ch: pl.BlockSpec((1, 1, 1, rows, DIL_GW), lambda b, r, i: (ch, b, r, i, 0))
    prev = lambda ch: pl.BlockSpec((1, 1, 1, DIL_NK, DIL_GW),
                                   lambda b, r, i: (ch, b, r, jnp.maximum(i * nb - 1, 0), 0))
    return pl.pallas_call(
        functools.partial(_dil_attn_kernel, rows=rows),
        grid=(B, dil, M // rows),
        in_specs=[cur(0), cur(1), prev(1), cur(2), prev(2)],
        out_specs=[pl.BlockSpec((1, 1, rows, DIL_GW), lambda b, r, i: (b, r, i, 0)),
                   pl.BlockSpec((1, 1, rows, LANES), lambda b, r, i: (b, r, i, 0))],
        out_shape=[jax.ShapeDtypeStruct((B, dil, M, DIL_GW), BF16),
                   jax.ShapeDtypeStruct((B, dil, M, LANES), F32)],
        scratch_shapes=[pltpu.VMEM((rows + DIL_NK, DIL_GW), BF16),
                        pltpu.VMEM((rows + DIL_NK, DIL_GW), BF16)],
        compiler_params=pltpu.CompilerParams(
            dimension_semantics=("parallel", "parallel", "arbitrary"),
            vmem_limit_bytes=VMEM_LIMIT),
        name=f"dil_attn_g{g}",
    )(qkv, qkv, qkv, qkv, qkv)


def _token_major(ref, dil, scr):
    n = ref.shape[2]
    nslab = ref.shape[3] // LANES
    if dil == 1:
        return ref[0, 0].astype(F32)
    for r in range(dil):
        blk = ref[0, r].astype(F32)
        for c in range(nslab):
            scr[c, pl.ds(r, n, stride=dil), :] = blk[:, c * LANES:(c + 1) * LANES]
    return jnp.concatenate([scr[c] for c in range(nslab)], axis=-1)


def _merge_kernel(x_ref, ylru_ref, omla_ref, od0_ref, od1_ref, od2_ref, l0_ref, l1_ref, l2_ref,
                  ng_ref, wf_ref, bm_ref, plru_ref, pmla_ref, pdil_ref, wout_ref, exp_ref, out_ref,
                  o_scr, l_scr):
    x = x_ref[0]
    h = _rms(x, ng_ref[...]).astype(BF16)

    def proj(lo, width):
        return jnp.dot(h, wf_ref[:, lo:lo + width], preferred_element_type=F32)

    g_off = N_BRANCH * D_MODEL
    mla_gate = proj(g_off, MLA_WIDTH)
    dil_gate = proj(g_off + MLA_WIDTH, DIL_WIDTH)
    y_mla = (omla_ref[0].astype(F32) * (mla_gate * jax.nn.sigmoid(mla_gate))).astype(BF16)

    od_refs = (od0_ref, od1_ref, od2_ref)
    l_refs = (l0_ref, l1_ref, l2_ref)
    lses = [_token_major(l_refs[g], dil, l_scr.at[g]) for g, (_, dil) in enumerate(DIL_GROUPS)]
    mx = jnp.maximum(jnp.maximum(lses[0], lses[1]), lses[2])
    es = [jnp.exp(l - mx) for l in lses]
    den = es[0] + es[1] + es[2]
    o_dil = None
    for g, (_, dil) in enumerate(DIL_GROUPS):
        w = jnp.dot((es[g] / den).astype(BF16), exp_ref[...], preferred_element_type=F32)
        term = w * _token_major(od_refs[g], dil, o_scr.at[g])
        o_dil = term if o_dil is None else o_dil + term
    y_dil = (o_dil * (dil_gate * jax.nn.sigmoid(dil_gate))).astype(BF16)

    ys = [ylru_ref[0], y_mla, y_dil]
    ps = [plru_ref, pmla_ref, pdil_ref]
    merged = None
    for br in range(N_BRANCH):
        gate = jax.nn.sigmoid(proj(br * D_MODEL, D_MODEL) + bm_ref[:, br * D_MODEL:(br + 1) * D_MODEL])
        term = gate * jnp.dot(ys[br], ps[br][...], preferred_element_type=F32)
        merged = term if merged is None else merged + term
    out_ref[0] = x + jnp.dot(merged.astype(BF16), wout_ref[...], preferred_element_type=F32)


def _merge_call(x, ylru, omla, ods, lses, ng, wf, bm, plru, pmla, pdil, wout, expand, tm):
    B, S, _ = x.shape
    tok = lambda w: pl.BlockSpec((1, tm, w), lambda b, i: (b, i, 0))
    res = lambda dil, w: pl.BlockSpec((1, dil, tm // dil, w), lambda b, i: (b, 0, i, 0))
    dils = [dil for _, dil in DIL_GROUPS]
    consts = (ng, wf, bm, plru, pmla, pdil, wout, expand)
    ngrp = len(DIL_GROUPS)
    return pl.pallas_call(
        _merge_kernel,
        grid=(B, S // tm),
        in_specs=[tok(D_MODEL), tok(LRU_WIDTH), tok(MLA_WIDTH)]
        + [res(d, DIL_GW) for d in dils] + [res(d, LANES) for d in dils]
        + [_const_spec(a.shape) for a in consts],
        out_specs=tok(D_MODEL),
        out_shape=jax.ShapeDtypeStruct((B, S, D_MODEL), F32),
        scratch_shapes=[pltpu.VMEM((ngrp, DIL_GW // LANES, tm, LANES), F32),
                        pltpu.VMEM((ngrp, 1, tm, LANES), F32)],
        compiler_params=pltpu.CompilerParams(
            dimension_semantics=("parallel", "parallel"), vmem_limit_bytes=VMEM_LIMIT),
        name="merge",
    )(x, ylru, omla, *ods, *lses, *consts)


def _mla_lanes(w):
    half = MLA_ROPE // 2
    pad = jnp.zeros(w.shape[:-1] + (LANES - MLA_QK,), w.dtype)
    out = jnp.concatenate([w[..., :48], w[..., MLA_NOPE:MLA_NOPE + half], w[..., 48:MLA_NOPE], pad,
                           w[..., MLA_NOPE + half:]], axis=-1)
    assert out.shape[-1] == LANES
    return out


def _mla_head_lanes(w, heads_feats):
    rows = w.shape[0]
    w = w.reshape(rows, MLA_HEADS, heads_feats)
    if heads_feats < MLA_QK:
        w = jnp.concatenate([w, jnp.zeros((rows, MLA_HEADS, MLA_QK - heads_feats), w.dtype)], axis=-1)
    return _mla_lanes(w).reshape(rows, MLA_HEADS * LANES)


def _mla_gain_lanes(g):
    return jnp.concatenate([_mla_lanes(g)] * MLA_HEADS)[None, :]


def _pair_layout(w):
    lead = w.shape[:-1]
    w = w.reshape(lead + (DIL_HEADS // 2, 2, 2, DIL_HD // 2))
    return jnp.swapaxes(w, -3, -2).reshape(lead + (DIL_GW,))


def kernel(x, positions, norm_g, w_in, conv_w, conv_b, w_gate_x, b_gate_x, w_gate_a, b_gate_a,
           lru_lambda, w_lru_o, cq_norm_g, ckv_norm_g, w_uq, w_ukv, mla_q_norm_g, mla_k_norm_g,
           w_mla_o, dil_q_norm_g, dil_k_norm_g, w_dil_o, b_merge, w_out):
    B, S, _ = x.shape
    tm = min(512, S)
    cos_d, sin_d, cos_m, sin_m = _rope_tables(positions, tm)
    dil_ones = jnp.asarray(_DIL_ONES, BF16)
    mla_ones = jnp.asarray(_MLA_ONES, BF16)
    v_one = jnp.asarray(((np.arange(MLA_HEADS * LANES) // MLA_V + 1) // 2 % 2)[None, :], F32)
    expand = jnp.asarray(
        (np.arange(LANES)[:, None] == np.arange(DIL_GW)[None, :] // DIL_HD).astype(np.float32), BF16)

    for l in range(DEPTH):
        wl = w_in[l]
        ng = norm_g[l][None, :]

        y_lru = _lru_call(
            x, ng, wl[:, OFF_LRU_X:OFF_CQ].astype(BF16), conv_w[l], conv_b[l][None, :],
            w_gate_x[l].astype(BF16), b_gate_x[l][:, None, :],
            w_gate_a[l].astype(BF16), b_gate_a[l][:, None, :], lru_lambda[l][None, :], tm)

        w_kr = _mla_lanes(jnp.concatenate(
            [jnp.zeros((D_MODEL, MLA_NOPE), F32), wl[:, OFF_KR:OFF_MLA_G]], axis=1))
        wb = jnp.concatenate([wl[:, OFF_CQ:OFF_KR], w_kr], axis=1).astype(BF16)
        wuq = _mla_head_lanes(w_uq[l], MLA_QK).astype(BF16)
        w_kv = w_ukv[l].reshape(KV_LORA, MLA_HEADS, MLA_NOPE + MLA_V)
        wuk = _mla_head_lanes(w_kv[..., :MLA_NOPE].reshape(KV_LORA, MLA_HEADS * MLA_NOPE),
                              MLA_NOPE).astype(BF16)
        wuv = w_ukv[l].reshape(KV_LORA, MLA_HEADS // 2, 2, MLA_NOPE + MLA_V)[..., MLA_NOPE:]
        zv = jnp.zeros_like(wuv[:, :, 0])
        wuv = jnp.stack([wuv[:, :, 0], zv, zv, wuv[:, :, 1]], axis=2)
        wuv = wuv.reshape(KV_LORA, MLA_HEADS * LANES).astype(BF16)
        q, k, v = _mla_prep_call(
            x, ng, wb, cq_norm_g[l][None, :], ckv_norm_g[l][None, :], wuq, wuk, wuv,
            _mla_gain_lanes(mla_q_norm_g[l]), _mla_gain_lanes(mla_k_norm_g[l]),
            mla_ones, v_one, cos_m, sin_m, tm)
        o_mla = _mla_attn_call(q, k, v, min(tm, S // 2))

        chunks, gains = [], []
        gq = _pair_layout(jnp.concatenate([dil_q_norm_g[l]] * DIL_HEADS))
        gk = _pair_layout(jnp.concatenate([dil_k_norm_g[l]] * DIL_HEADS))
        for g in range(len(DIL_GROUPS)):
            chunks += [_pair_layout(wl[:, OFF_DQ + g * DIL_GW:OFF_DQ + (g + 1) * DIL_GW]),
                       _pair_layout(wl[:, OFF_DK + g * DIL_GW:OFF_DK + (g + 1) * DIL_GW]),
                       wl[:, OFF_DV + g * DIL_GW:OFF_DV + (g + 1) * DIL_GW]]
            gains += [gq, gk, jnp.ones_like(gq)]
        wd = jnp.stack(chunks).astype(BF16)
        gd = jnp.stack(gains)[:, None, :]
        qkvs = _dil_prep_call(x, ng, wd, gd, dil_ones, cos_d, sin_d, tm)
        ods, lses = [], []
        for g, (window, dil) in enumerate(DIL_GROUPS):
            assert window // dil == DIL_NK
            o_g, lse_g = _dil_attn_call(qkvs[g], g, min(2 * DIL_NK, S // dil))
            ods.append(o_g)
            lses.append(lse_g)

        wf = jnp.concatenate([wl[:, OFF_MERGE:], wl[:, OFF_MLA_G:OFF_DQ], wl[:, OFF_DIL_G:OFF_MERGE]],
                             axis=1).astype(BF16)
        x = _merge_call(x, y_lru, o_mla, ods, lses, ng, wf, b_merge[l][None, :],
                        w_lru_o[l].astype(BF16), w_mla_o[l].astype(BF16), w_dil_o[l].astype(BF16),
                        w_out[l].astype(BF16), expand, tm)
    return x
```

```python
import functools
import math

import numpy as np
import jax
import jax.numpy as jnp
from jax import lax
from jax.experimental import pallas as pl
from jax.experimental.pallas import tpu as pltpu

F32 = jnp.float32
BF16 = jnp.bfloat16

D_MODEL = 1024
DEPTH = 2
EPS = 1e-6
ROPE_THETA = 10000.0

LRU_WIDTH = 1024
LRU_BLOCKS = 8
LRU_BLOCK_W = LRU_WIDTH // LRU_BLOCKS
CONV_WIDTH = 4
LRU_C = 8.0

MLA_HEADS = 8
MLA_NOPE = 64
MLA_ROPE = 32
MLA_QK = MLA_NOPE + MLA_ROPE
MLA_V = 64
Q_LORA = 256
KV_LORA = 128
MLA_WIDTH = MLA_HEADS * MLA_V

DIL_GROUPS = ((128, 1), (512, 4), (2048, 16))
DIL_HEADS = 8
DIL_HD = 64
DIL_GW = DIL_HEADS * DIL_HD
DIL_QKV = len(DIL_GROUPS) * DIL_GW
DIL_WIDTH = DIL_GW
DIL_NK = 128
DIL_ROWS = 8 * DIL_NK

N_BRANCH = 3
OFF_LRU_X = 0
OFF_LRU_G = OFF_LRU_X + LRU_WIDTH
OFF_CQ = OFF_LRU_G + LRU_WIDTH
OFF_CKV = OFF_CQ + Q_LORA
OFF_KR = OFF_CKV + KV_LORA
OFF_MLA_G = OFF_KR + MLA_ROPE
OFF_DQ = OFF_MLA_G + MLA_WIDTH
OFF_DK = OFF_DQ + DIL_QKV
OFF_DV = OFF_DK + DIL_QKV
OFF_DIL_G = OFF_DV + DIL_QKV
OFF_MERGE = OFF_DIL_G + DIL_WIDTH

LANES = 128
SUBLANES = 8
VMEM_LIMIT = 48 * 1024 * 1024

NEG_INF = float("-inf")
LOG2E = math.log2(math.e)
LN2 = math.log(2.0)


def _mla_lane_feat():
    feat = -np.ones((LANES,), np.int64)
    feat[0:48] = np.arange(48)
    feat[48:64] = MLA_NOPE + np.arange(16)
    feat[64:80] = 48 + np.arange(16)
    feat[112:128] = MLA_NOPE + 16 + np.arange(16)
    return feat


_MLA_FEAT = _mla_lane_feat()


def _block_ones(width, head_of_lane):
    m = (head_of_lane[:, None] == head_of_lane[None, :]).astype(np.float32)
    assert m.shape == (width, width)
    return m


_DIL_HEAD_OF_LANE = (np.arange(2 * LANES) // LANES) * 2 + (np.arange(2 * LANES) % LANES // 32) % 2
_DIL_ONES = _block_ones(2 * LANES, _DIL_HEAD_OF_LANE)
_MLA_ONES = _block_ones(2 * LANES, np.arange(2 * LANES) // LANES)


def _const_spec(shape):
    nd = len(shape)
    return pl.BlockSpec(shape, lambda *_: (0,) * nd, pipeline_mode=pl.Buffered(1))


def _rms(xf, g_row):
    y = xf * lax.rsqrt(jnp.mean(xf * xf, axis=-1, keepdims=True) + EPS)
    return y * g_row


def _head_rms_rope(z, ones_ref, head_dim, g_row, cos, sin):
    n = z.shape[-1]
    parts = []
    for c in range(n // (2 * LANES)):
        zc = z[:, c * 2 * LANES:(c + 1) * 2 * LANES]
        ss = jnp.dot((zc * zc).astype(BF16), ones_ref[...], preferred_element_type=F32)
        y = zc * lax.rsqrt(ss * (1.0 / head_dim) + EPS) * g_row[:, c * 2 * LANES:(c + 1) * 2 * LANES]
        for s in range(2):
            ys = y[:, s * LANES:(s + 1) * LANES]
            parts.append(ys * cos + pltpu.roll(ys, LANES // 2, 1) * sin)
    return jnp.concatenate(parts, axis=-1)


def _rope_kernel(pos_ref, inv_ref, sgn_ref, cd_ref, sd_ref, cm_ref, sm_ref):
    pos = pos_ref[0].astype(F32)
    for t, (c_ref, s_ref) in enumerate(((cd_ref, sd_ref), (cm_ref, sm_ref))):
        ang = pos * inv_ref[t:t + 1, :]
        c_ref[0] = jnp.cos(ang)
        s_ref[0] = jnp.sin(ang) * sgn_ref[t:t + 1, :]


def _rope_tables(positions, tm):
    B, S = positions.shape
    inv_d = ROPE_THETA ** (-jnp.arange(0, DIL_HD, 2, dtype=F32) / DIL_HD)
    inv_m = ROPE_THETA ** (-jnp.arange(0, MLA_ROPE, 2, dtype=F32) / MLA_ROPE)
    lane = np.arange(LANES)
    inv_d_l = inv_d[lane % (DIL_HD // 2)]
    sgn_d_l = np.where(lane < LANES // 2, -1.0, 1.0)
    rope_feat = _MLA_FEAT - MLA_NOPE
    is_rope = rope_feat >= 0
    inv_m_l = jnp.where(is_rope, inv_m[np.where(is_rope, rope_feat % (MLA_ROPE // 2), 0)], 0.0)
    sgn_m_l = np.where(is_rope, np.where(rope_feat < MLA_ROPE // 2, -1.0, 1.0), 0.0)
    inv = jnp.stack([inv_d_l, inv_m_l]).astype(F32)
    sgn = jnp.asarray(np.stack([sgn_d_l, sgn_m_l]), F32)
    out = jax.ShapeDtypeStruct((B, S, LANES), F32)
    tab = pl.BlockSpec((1, tm, LANES), lambda b, i: (b, i, 0))
    return pl.pallas_call(
        _rope_kernel,
        grid=(B, S // tm),
        in_specs=[pl.BlockSpec((1, tm, 1), lambda b, i: (b, i, 0)),
                  _const_spec((2, LANES)), _const_spec((2, LANES))],
        out_specs=[tab, tab, tab, tab],
        out_shape=[out, out, out, out],
        compiler_params=pltpu.CompilerParams(
            dimension_semantics=("parallel", "parallel"), vmem_limit_bytes=VMEM_LIMIT),
        name="rope_tables",
    )(positions.reshape(B, S, 1), inv, sgn)


def _lru_kernel(x_ref, ng_ref, w_ref, cw_ref, cb_ref, wgx_ref, bgx_ref, wga_ref, bga_ref,
                lam_ref, perm_ref, unperm_ref, y_ref, xbuf, a_scr, b_scr, tail, hcar, *, ts):
    t = pl.program_id(1)
    nchunk = SUBLANES
    clen = ts // nchunk
    halo = (CONV_WIDTH - 1) * SUBLANES

    @pl.when(t == 0)
    def _():
        tail[...] = jnp.zeros(tail.shape, F32)
        hcar[...] = jnp.zeros((1, LRU_WIDTH), F32)

    h = _rms(x_ref[0], ng_ref[...]).astype(BF16)
    h = jnp.dot(perm_ref[...], h, preferred_element_type=F32).astype(BF16)
    z = jnp.dot(h, w_ref[...], preferred_element_type=F32)
    lx = z[:, :LRU_WIDTH]
    gate = z[:, LRU_WIDTH:]
    xbuf[halo:halo + ts, :] = lx
    last = lx[ts - halo:, :]
    shifted = jnp.concatenate(
        [pltpu.roll(last[g * SUBLANES:(g + 1) * SUBLANES], 1, 0) for g in range(CONV_WIDTH - 1)], axis=0)
    prev_tail = jnp.concatenate(
        [jnp.broadcast_to(tail[g:g + 1, :], (SUBLANES, LRU_WIDTH)) for g in range(CONV_WIDTH - 1)], axis=0)
    sub = lax.broadcasted_iota(jnp.int32, (halo, LRU_WIDTH), 0) % SUBLANES
    xbuf[0:halo, :] = jnp.where(sub == 0, prev_tail, shifted)
    for g in range(CONV_WIDTH - 1):
        tail[g:g + 1, :] = last[g * SUBLANES + nchunk - 1:g * SUBLANES + nchunk, :]
    xc = cb_ref[...]
    for k in range(CONV_WIDTH):
        lo = k * SUBLANES
        xc = xc + xbuf[lo:lo + ts, :] * cw_ref[k:k + 1, :]

    lam = lam_ref[...]
    neg_sp = -(jnp.maximum(-lam, 0.0) + jnp.log1p(jnp.exp(-jnp.abs(lam))))
    for n in range(LRU_BLOCKS):
        sl = slice(n * LRU_BLOCK_W, (n + 1) * LRU_BLOCK_W)
        xb = xc[:, sl]
        xb16 = xb.astype(BF16)
        gx = jax.nn.sigmoid(jnp.dot(xb16, wgx_ref[n], preferred_element_type=F32) + bgx_ref[n])
        ga = jax.nn.sigmoid(jnp.dot(xb16, wga_ref[n], preferred_element_type=F32) + bga_ref[n])
        log_a = LRU_C * ga * neg_sp[:, sl]
        a = jnp.exp(log_a)
        one_m_a2 = -jnp.tanh(log_a) * (1.0 + a * a)
        mult = jnp.where(one_m_a2 > 0.0, one_m_a2 * lax.rsqrt(one_m_a2), 0.0)
        a_scr[:, sl] = a
        b_scr[:, sl] = mult * (gx * xb)

    h_end = jnp.zeros((SUBLANES, LRU_WIDTH), F32)
    a_end = jnp.ones((SUBLANES, LRU_WIDTH), F32)
    for g in range(clen):
        r = g * SUBLANES
        a8 = a_scr[r:r + SUBLANES, :]
        h_end = a8 * h_end + b_scr[r:r + SUBLANES, :]
        a_end = a8 * a_end
        b_scr[r:r + SUBLANES, :] = h_end
        a_scr[r:r + SUBLANES, :] = a_end
    state = hcar[...]
    starts = []
    for j in range(nchunk):
        starts.append(state)
        state = h_end[j:j + 1, :] + a_end[j:j + 1, :] * state
    hcar[...] = state
    start = jnp.concatenate(starts, axis=0)
    hl = b_scr[...].reshape(clen, SUBLANES, LRU_WIDTH)
    ap = a_scr[...].reshape(clen, SUBLANES, LRU_WIDTH)
    hfull = (hl + ap * start[None]).reshape(ts, LRU_WIDTH)
    y = (hfull * (gate * jax.nn.sigmoid(gate))).astype(BF16)
    y_ref[0] = jnp.dot(unperm_ref[...], y, preferred_element_type=F32).astype(BF16)


def _lru_call(x, ng, w, cw, cb, wgx, bgx, wga, bga, lam, ts):
    B, S, _ = x.shape
    clen = ts // SUBLANES
    rho = np.arange(ts)
    step_of_row = (rho % SUBLANES) * clen + rho // SUBLANES
    perm_np = (step_of_row[:, None] == np.arange(ts)[None, :]).astype(np.float32)
    perm = jnp.asarray(perm_np, BF16)
    unperm = jnp.asarray(perm_np.T, BF16)
    return pl.pallas_call(
        functools.partial(_lru_kernel, ts=ts),
        grid=(B, S // ts),
        in_specs=[pl.BlockSpec((1, ts, D_MODEL), lambda b, t: (b, t, 0)),
                  _const_spec(ng.shape), _const_spec(w.shape), _const_spec(cw.shape),
                  _const_spec(cb.shape), _const_spec(wgx.shape), _const_spec(bgx.shape),
                  _const_spec(wga.shape), _const_spec(bga.shape), _const_spec(lam.shape),
                  _const_spec(perm.shape), _const_spec(unperm.shape)],
        out_specs=pl.BlockSpec((1, ts, LRU_WIDTH), lambda b, t: (b, t, 0)),
        out_shape=jax.ShapeDtypeStruct((B, S, LRU_WIDTH), BF16),
        scratch_shapes=[pltpu.VMEM((ts + (CONV_WIDTH - 1) * SUBLANES, LRU_WIDTH), F32),
                        pltpu.VMEM((ts, LRU_WIDTH), F32),
                        pltpu.VMEM((ts, LRU_WIDTH), F32),
                        pltpu.VMEM((SUBLANES, LRU_WIDTH), F32),
                        pltpu.VMEM((1, LRU_WIDTH), F32)],
        compiler_params=pltpu.CompilerParams(
            dimension_semantics=("parallel", "arbitrary"), vmem_limit_bytes=VMEM_LIMIT),
        name="lru_branch",
    )(x, ng, w, cw, cb, wgx, bgx, wga, bga, lam, perm, unperm)


def _mla_prep_kernel(x_ref, ng_ref, wb_ref, gcq_ref, gckv_ref, wuq_ref, wuk_ref, wuv_ref,
                     gq_ref, gk_ref, ones_ref, vone_ref, cos_ref, sin_ref, q_ref, k_ref, v_ref):
    h = _rms(x_ref[0], ng_ref[...]).astype(BF16)
    z = jnp.dot(h, wb_ref[...], preferred_element_type=F32)
    cq = _rms(z[:, :Q_LORA], gcq_ref[...]).astype(BF16)
    ckv = _rms(z[:, Q_LORA:Q_LORA + KV_LORA], gckv_ref[...]).astype(BF16)
    kr = z[:, Q_LORA + KV_LORA:]
    cos = cos_ref[0]
    sin = sin_ref[0]
    q = jnp.dot(cq, wuq_ref[...], preferred_element_type=F32)
    q = _head_rms_rope(q, ones_ref, MLA_QK, gq_ref[...], cos, sin) * (MLA_QK ** -0.5 * LOG2E)
    k = jnp.dot(ckv, wuk_ref[...], preferred_element_type=F32)
    k = k + jnp.concatenate([kr] * MLA_HEADS, axis=-1)
    k = _head_rms_rope(k, ones_ref, MLA_QK, gk_ref[...], cos, sin)
    for hh in range(MLA_HEADS):
        q_ref[0, hh] = q[:, hh * LANES:(hh + 1) * LANES].astype(BF16)
        k_ref[0, hh] = k[:, hh * LANES:(hh + 1) * LANES].astype(BF16)
    v = jnp.dot(ckv, wuv_ref[...], preferred_element_type=F32) + vone_ref[...]
    for hh in range(MLA_HEADS):
        v_ref[0, hh] = v[:, hh * LANES:(hh + 1) * LANES].astype(BF16)


def _mla_prep_call(x, ng, wb, gcq, gckv, wuq, wuk, wuv, gq, gk, ones, vone, cos_t, sin_t, tm):
    B, S, _ = x.shape
    tab = pl.BlockSpec((1, tm, LANES), lambda b, i: (b, i, 0))
    consts = (ng, wb, gcq, gckv, wuq, wuk, wuv, gq, gk, ones, vone)
    return pl.pallas_call(
        _mla_prep_kernel,
        grid=(B, S // tm),
        in_specs=[pl.BlockSpec((1, tm, D_MODEL), lambda b, i: (b, i, 0))]
        + [_const_spec(a.shape) for a in consts] + [tab, tab],
        out_specs=[pl.BlockSpec((1, MLA_HEADS, tm, LANES), lambda b, i: (b, 0, i, 0))] * 3,
        out_shape=[jax.ShapeDtypeStruct((B, MLA_HEADS, S, LANES), BF16)] * 3,
        compiler_params=pltpu.CompilerParams(
            dimension_semantics=("parallel", "parallel"), vmem_limit_bytes=VMEM_LIMIT),
        name="mla_prep",
    )(x, *consts, cos_t, sin_t)


def _mla_attn_kernel(q_ref, k_ref, v_ref, o_ref, *, tq, tk):
    qi = pl.program_id(2)
    row = lax.broadcasted_iota(jnp.int32, (tk, tk), 0)
    col = lax.broadcasted_iota(jnp.int32, (tk, tk), 1)
    causal = col <= row

    def update(state, s, v):
        m, acc = state
        m_new = jnp.maximum(m, jnp.max(s, axis=-1, keepdims=True))
        p = jnp.exp2((s - m_new).astype(BF16))
        return m_new, jnp.exp2(m - m_new) * acc + jnp.dot(p, v, preferred_element_type=F32)

    def kv(hh, j):
        start = pl.multiple_of(j * tk, tk)
        return k_ref[0, hh, pl.ds(start, tk), :], v_ref[0, hh, pl.ds(start, tk), :]

    def scores(qv, k):
        return lax.dot_general(qv, k, (((1,), (1,)), ((), ())), preferred_element_type=F32)

    def step(j, carry):
        new = []
        for hh in range(2):
            k, v = kv(hh, j)
            s = scores(q_ref[0, hh], k)
            new.append((update(carry[hh][0], s[:tk], v), update(carry[hh][1], s[tk:], v)))
        return tuple(new)

    def init_half():
        return jnp.full((tk, 1), NEG_INF, F32), jnp.zeros((tk, LANES), F32)

    carry = lax.fori_loop(0, qi, lambda t, c: step(2 * t + 1, step(2 * t, c)),
                          tuple((init_half(), init_half()) for _ in range(2)))
    outs = []
    for hh in range(2):
        k, v = kv(hh, 2 * qi)
        s = scores(q_ref[0, hh], k)
        _, acc_a = update(carry[hh][0], jnp.where(causal, s[:tk], NEG_INF), v)
        half_b = update(carry[hh][1], s[tk:], v)
        k, v = kv(hh, 2 * qi + 1)
        s = scores(q_ref[0, hh, tk:, :], k)
        _, acc_b = update(half_b, jnp.where(causal, s, NEG_INF), v)
        outs.append(jnp.concatenate([acc_a, acc_b], axis=0))
    lane = lax.broadcasted_iota(jnp.int32, (tq, LANES), 1)
    o = jnp.where(lane < MLA_V, outs[0] / pltpu.roll(outs[0], MLA_V, 1),
                  outs[1] / pltpu.roll(outs[1], MLA_V, 1))
    o_ref[0] = o.astype(BF16)


def _mla_attn_call(q, k, v, tk):
    B, H, S, _ = q.shape
    tq = 2 * tk
    return pl.pallas_call(
        functools.partial(_mla_attn_kernel, tq=tq, tk=tk),
        grid=(B, H // 2, S // tq),
        in_specs=[pl.BlockSpec((1, 2, tq, LANES), lambda b, p, i: (b, p, i, 0)),
                  pl.BlockSpec((1, 2, S, LANES), lambda b, p, i: (b, p, 0, 0)),
                  pl.BlockSpec((1, 2, S, LANES), lambda b, p, i: (b, p, 0, 0))],
        out_specs=pl.BlockSpec((1, tq, LANES), lambda b, p, i: (b, i, p)),
        out_shape=jax.ShapeDtypeStruct((B, S, MLA_WIDTH), BF16),
        compiler_params=pltpu.CompilerParams(
            dimension_semantics=("parallel", "parallel", "arbitrary"),
            vmem_limit_bytes=VMEM_LIMIT),
        name="mla_attn",
    )(q, k, v)


def _residue_major(ref_at, dil, n):
    if dil == 1:
        return ref_at[...]
    return jnp.concatenate([ref_at[pl.ds(r, n, stride=dil), :] for r in range(dil)], axis=0)


def _dil_prep_kernel(x_ref, ng_ref, w_ref, g_ref, ones_ref, cos_ref, sin_ref,
                     o0_ref, o1_ref, o2_ref, xs_scr, *, tm):
    xn = _rms(x_ref[0], ng_ref[...])
    nslab = D_MODEL // LANES
    for c in range(nslab):
        xs_scr[c] = xn[:, c * LANES:(c + 1) * LANES]
    outs = (o0_ref, o1_ref, o2_ref)
    for g, (_, dil) in enumerate(DIL_GROUPS):
        n = tm // dil
        h = jnp.concatenate([_residue_major(xs_scr.at[c], dil, n) for c in range(nslab)],
                            axis=-1).astype(BF16)
        cos = _residue_major(cos_ref.at[0], dil, n)
        sin = _residue_major(sin_ref.at[0], dil, n)
        for j in range(3):
            z = jnp.dot(h, w_ref[3 * g + j], preferred_element_type=F32)
            if j < 2:
                z = _head_rms_rope(z, ones_ref, DIL_HD, g_ref[3 * g + j], cos, sin)
            if j == 0:
                z = z * (DIL_HD ** -0.5 * LOG2E)
            z = z.astype(BF16)
            for r in range(dil):
                outs[g][j, 0, r] = z[r * n:(r + 1) * n]


def _dil_prep_call(x, ng, wd, gd, ones, cos_t, sin_t, tm):
    B, S, _ = x.shape
    tab = pl.BlockSpec((1, tm, LANES), lambda b, i: (b, i, 0))
    out_specs, out_shape = [], []
    for _, dil in DIL_GROUPS:
        out_specs.append(pl.BlockSpec((3, 1, dil, tm // dil, DIL_GW), lambda b, i: (0, b, 0, i, 0)))
        out_shape.append(jax.ShapeDtypeStruct((3, B, dil, S // dil, DIL_GW), BF16))
    return pl.pallas_call(
        functools.partial(_dil_prep_kernel, tm=tm),
        grid=(B, S // tm),
        in_specs=[pl.BlockSpec((1, tm, D_MODEL), lambda b, i: (b, i, 0)),
                  _const_spec(ng.shape), _const_spec(wd.shape), _const_spec(gd.shape),
                  _const_spec(ones.shape), tab, tab],
        out_specs=out_specs,
        out_shape=out_shape,
        scratch_shapes=[pltpu.VMEM((D_MODEL // LANES, tm, LANES), F32)],
        compiler_params=pltpu.CompilerParams(
            dimension_semantics=("parallel", "parallel"), vmem_limit_bytes=VMEM_LIMIT),
        name="dil_prep",
    )(x, ng, wd, gd, ones, cos_t, sin_t)


def _dil_attn_kernel(q_ref, kc_ref, kp_ref, vc_ref, vp_ref, o_ref, lse_ref, kbuf, vbuf, *, rows):
    i = pl.program_id(2)
    nk = DIL_NK
    kbuf[0:nk, :] = kp_ref[0, 0, 0]
    kbuf[nk:nk + rows, :] = kc_ref[0, 0, 0]
    vbuf[0:nk, :] = vp_ref[0, 0, 0]
    vbuf[nk:nk + rows, :] = vc_ref[0, 0, 0]

    qi = lax.broadcasted_iota(jnp.int32, (2 * nk, 2 * nk), 0) % nk
    ki = lax.broadcasted_iota(jnp.int32, (2 * nk, 2 * nk), 1)
    band = (ki >= qi) & (ki <= qi + nk)
    band_first = band & (ki >= jnp.where(i > 0, 0, nk))
    lane = lax.broadcasted_iota(jnp.int32, (nk, LANES), 1)
    head_a = (lane // (DIL_HD // 2)) % 2 == 0

    for n in range(rows // nk):
        mask = band_first if n == 0 else band
        lse_acc = jnp.zeros((nk, LANES), F32)
        for p in range(DIL_HEADS // 2):
            cs = slice(p * LANES, (p + 1) * LANES)
            qp = q_ref[0, 0, 0, n * nk:(n + 1) * nk, cs]
            zero = jnp.zeros_like(qp)
            q2 = jnp.concatenate([jnp.where(head_a, qp, zero), jnp.where(head_a, zero, qp)], axis=0)
            kk = kbuf[n * nk:(n + 2) * nk, cs]
            vv = vbuf[n * nk:(n + 2) * nk, cs]
            s = lax.dot_general(q2, kk, (((1,), (1,)), ((), ())), preferred_element_type=F32)
            s = jnp.where(mask, s, NEG_INF)
            m = jnp.max(s, axis=-1, keepdims=True)
            e = jnp.exp2(s - m)
            den = jnp.sum(e, axis=-1, keepdims=True)
            pv = jnp.dot(e.astype(BF16), vv, preferred_element_type=F32) / den
            lse = (m + jnp.log2(den)) * LN2
            o_ref[0, 0, n * nk:(n + 1) * nk, cs] = jnp.where(
                lane < DIL_HD, pv[:nk], pv[nk:]).astype(o_ref.dtype)
            lse_acc = jnp.where(lane == 2 * p, lse[:nk], lse_acc)
            lse_acc = jnp.where(lane == 2 * p + 1, lse[nk:], lse_acc)
        lse_ref[0, 0, n * nk:(n + 1) * nk, :] = lse_acc


def _dil_attn_call(qkv, g, rows):
    _, B, dil, M, _ = qkv.shape
    nb = rows // DIL_NK
    cur = lambda ch: pl.BlockSpec((1, 1, 1, rows, DIL_GW), lambda b, r, i: (ch, b, r, i, 0))
    prev = lambda ch: pl.BlockSpec((1, 1, 1, DIL_NK, DIL_GW),
                                   lambda b, r, i: (ch, b, r, jnp.maximum(i * nb - 1, 0), 0))
    return pl.pallas_call(
        functools.partial(_dil_attn_kernel, rows=rows),
        grid=(B, dil, M // rows),
        in_specs=[cur(0), cur(1), prev(1), cur(2), prev(2)],
        out_specs=[pl.BlockSpec((1, 1, rows, DIL_GW), lambda b, r, i: (b, r, i, 0)),
                   pl.BlockSpec((1, 1, rows, LANES), lambda b, r, i: (b, r, i, 0))],
        out_shape=[jax.ShapeDtypeStruct((B, dil, M, DIL_GW), BF16),
                   jax.ShapeDtypeStruct((B, dil, M, LANES), F32)],
        scratch_shapes=[pltpu.VMEM((rows + DIL_NK, DIL_GW), BF16),
                        pltpu.VMEM((rows + DIL_NK, DIL_GW), BF16)],
        compiler_params=pltpu.CompilerParams(
            dimension_semantics=("parallel", "parallel", "arbitrary"),
            vmem_limit_bytes=VMEM_LIMIT),
        name=f"dil_attn_g{g}",
    )(qkv, qkv, qkv, qkv, qkv)


def _token_major(ref, dil, scr):
    n = ref.shape[2]
    nslab = ref.shape[3] // LANES
    if dil == 1:
        return ref[0, 0].astype(F32)
    for r in range(dil):
        blk = ref[0, r].astype(F32)
        for c in range(nslab):
            scr[c, pl.ds(r, n, stride=dil), :] = blk[:, c * LANES:(c + 1) * LANES]
    return jnp.concatenate([scr[c] for c in range(nslab)], axis=-1)


def _merge_kernel(x_ref, ylru_ref, omla_ref, od0_ref, od1_ref, od2_ref, l0_ref, l1_ref, l2_ref,
                  ng_ref, wf_ref, bm_ref, plru_ref, pmla_ref, pdil_ref, wout_ref, exp_ref, out_ref,
                  o_scr, l_scr):
    x = x_ref[0]
    h = _rms(x, ng_ref[...]).astype(BF16)

    def proj(lo, width):
        return jnp.dot(h, wf_ref[:, lo:lo + width], preferred_element_type=F32)

    g_off = N_BRANCH * D_MODEL
    mla_gate = proj(g_off, MLA_WIDTH)
    dil_gate = proj(g_off + MLA_WIDTH, DIL_WIDTH)
    y_mla = (omla_ref[0].astype(F32) * (mla_gate * jax.nn.sigmoid(mla_gate))).astype(BF16)

    od_refs = (od0_ref, od1_ref, od2_ref)
    l_refs = (l0_ref, l1_ref, l2_ref)
    lses = [_token_major(l_refs[g], dil, l_scr.at[g]) for g, (_, dil) in enumerate(DIL_GROUPS)]
    mx = jnp.maximum(jnp.maximum(lses[0], lses[1]), lses[2])
    es = [jnp.exp(l - mx) for l in lses]
    den = es[0] + es[1] + es[2]
    o_dil = None
    for g, (_, dil) in enumerate(DIL_GROUPS):
        w = jnp.dot((es[g] / den).astype(BF16), exp_ref[...], preferred_element_type=F32)
        term = w * _token_major(od_refs[g], dil, o_scr.at[g])
        o_dil = term if o_dil is None else o_dil + term
    y_dil = (o_dil * (dil_gate * jax.nn.sigmoid(dil_gate))).astype(BF16)

    ys = [ylru_ref[0], y_mla, y_dil]
    ps = [plru_ref, pmla_ref, pdil_ref]
    merged = None
    for br in range(N_BRANCH):
        gate = jax.nn.sigmoid(proj(br * D_MODEL, D_MODEL) + bm_ref[:, br * D_MODEL:(br + 1) * D_MODEL])
        term = gate * jnp.dot(ys[br], ps[br][...], preferred_element_type=F32)
        merged = term if merged is None else merged + term
    out_ref[0] = x + jnp.dot(merged.astype(BF16), wout_ref[...], preferred_element_type=F32)


def _merge_call(x, ylru, omla, ods, lses, ng, wf, bm, plru, pmla, pdil, wout, expand, tm):
    B, S, _ = x.shape
    tok = lambda w: pl.BlockSpec((1, tm, w), lambda b, i: (b, i, 0))
    res = lambda dil, w: pl.BlockSpec((1, dil, tm // dil, w), lambda b, i: (b, 0, i, 0))
    dils = [dil for _, dil in DIL_GROUPS]
    consts = (ng, wf, bm, plru, pmla, pdil, wout, expand)
    ngrp = len(DIL_GROUPS)
    return pl.pallas_call(
        _merge_kernel,
        grid=(B, S // tm),
        in_specs=[tok(D_MODEL), tok(LRU_WIDTH), tok(MLA_WIDTH)]
        + [res(d, DIL_GW) for d in dils] + [res(d, LANES) for d in dils]
        + [_const_spec(a.shape) for a in consts],
        out_specs=tok(D_MODEL),
        out_shape=jax.ShapeDtypeStruct((B, S, D_MODEL), F32),
        scratch_shapes=[pltpu.VMEM((ngrp, DIL_GW // LANES, tm, LANES), F32),
                        pltpu.VMEM((ngrp, 1, tm, LANES), F32)],
        compiler_params=pltpu.CompilerParams(
            dimension_semantics=("parallel", "parallel"), vmem_limit_bytes=VMEM_LIMIT),
        name="merge",
    )(x, ylru, omla, *ods, *lses, *consts)


def _mla_lanes(w):
    half = MLA_ROPE // 2
    pad = jnp.zeros(w.shape[:-1] + (LANES - MLA_QK,), w.dtype)
    out = jnp.concatenate([w[..., :48], w[..., MLA_NOPE:MLA_NOPE + half], w[..., 48:MLA_NOPE], pad,
                           w[..., MLA_NOPE + half:]], axis=-1)
    assert out.shape[-1] == LANES
    return out


def _mla_head_lanes(w, heads_feats):
    rows = w.shape[0]
    w = w.reshape(rows, MLA_HEADS, heads_feats)
    if heads_feats < MLA_QK:
        w = jnp.concatenate([w, jnp.zeros((rows, MLA_HEADS, MLA_QK - heads_feats), w.dtype)], axis=-1)
    return _mla_lanes(w).reshape(rows, MLA_HEADS * LANES)


def _mla_gain_lanes(g):
    return jnp.concatenate([_mla_lanes(g)] * MLA_HEADS)[None, :]


def _pair_layout(w):
    lead = w.shape[:-1]
    w = w.reshape(lead + (DIL_HEADS // 2, 2, 2, DIL_HD // 2))
    return jnp.swapaxes(w, -3, -2).reshape(lead + (DIL_GW,))


def kernel(x, positions, norm_g, w_in, conv_w, conv_b, w_gate_x, b_gate_x, w_gate_a, b_gate_a,
           lru_lambda, w_lru_o, cq_norm_g, ckv_norm_g, w_uq, w_ukv, mla_q_norm_g, mla_k_norm_g,
           w_mla_o, dil_q_norm_g, dil_k_norm_g, w_dil_o, b_merge, w_out):
    B, S, _ = x.shape
    tm = min(512, S)
    cos_d, sin_d, cos_m, sin_m = _rope_tables(positions, tm)
    dil_ones = jnp.asarray(_DIL_ONES, BF16)
    mla_ones = jnp.asarray(_MLA_ONES, BF16)
    v_one = jnp.asarray(((np.arange(MLA_HEADS * LANES) // MLA_V + 1) // 2 % 2)[None, :], F32)
    expand = jnp.asarray(
        (np.arange(LANES)[:, None] == np.arange(DIL_GW)[None, :] // DIL_HD).astype(np.float32), BF16)

    for l in range(DEPTH):
        wl = w_in[l]
        ng = norm_g[l][None, :]

        y_lru = _lru_call(
            x, ng, wl[:, OFF_LRU_X:OFF_CQ].astype(BF16), conv_w[l], conv_b[l][None, :],
            w_gate_x[l].astype(BF16), b_gate_x[l][:, None, :],
            w_gate_a[l].astype(BF16), b_gate_a[l][:, None, :], lru_lambda[l][None, :], tm)

        w_kr = _mla_lanes(jnp.concatenate(
            [jnp.zeros((D_MODEL, MLA_NOPE), F32), wl[:, OFF_KR:OFF_MLA_G]], axis=1))
        wb = jnp.concatenate([wl[:, OFF_CQ:OFF_KR], w_kr], axis=1).astype(BF16)
        wuq = _mla_head_lanes(w_uq[l], MLA_QK).astype(BF16)
        w_kv = w_ukv[l].reshape(KV_LORA, MLA_HEADS, MLA_NOPE + MLA_V)
        wuk = _mla_head_lanes(w_kv[..., :MLA_NOPE].reshape(KV_LORA, MLA_HEADS * MLA_NOPE),
                              MLA_NOPE).astype(BF16)
        wuv = w_ukv[l].reshape(KV_LORA, MLA_HEADS // 2, 2, MLA_NOPE + MLA_V)[..., MLA_NOPE:]
        zv = jnp.zeros_like(wuv[:, :, 0])
        wuv = jnp.stack([wuv[:, :, 0], zv, zv, wuv[:, :, 1]], axis=2)
        wuv = wuv.reshape(KV_LORA, MLA_HEADS * LANES).astype(BF16)
        q, k, v = _mla_prep_call(
            x, ng, wb, cq_norm_g[l][None, :], ckv_norm_g[l][None, :], wuq, wuk, wuv,
            _mla_gain_lanes(mla_q_norm_g[l]), _mla_gain_lanes(mla_k_norm_g[l]),
            mla_ones, v_one, cos_m, sin_m, tm)
        o_mla = _mla_attn_call(q, k, v, min(tm, S // 2))

        chunks, gains = [], []
        gq = _pair_layout(jnp.concatenate([dil_q_norm_g[l]] * DIL_HEADS))
        gk = _pair_layout(jnp.concatenate([dil_k_norm_g[l]] * DIL_HEADS))
        for g in range(len(DIL_GROUPS)):
            chunks += [_pair_layout(wl[:, OFF_DQ + g * DIL_GW:OFF_DQ + (g + 1) * DIL_GW]),
                       _pair_layout(wl[:, OFF_DK + g * DIL_GW:OFF_DK + (g + 1) * DIL_GW]),
                       wl[:, OFF_DV + g * DIL_GW:OFF_DV + (g + 1) * DIL_GW]]
            gains += [gq, gk, jnp.ones_like(gq)]
        wd = jnp.stack(chunks).astype(BF16)
        gd = jnp.stack(gains)[:, None, :]
        qkvs = _dil_prep_call(x, ng, wd, gd, dil_ones, cos_d, sin_d, tm)
        ods, lses = [], []
        for g, (window, dil) in enumerate(DIL_GROUPS):
            assert window // dil == DIL_NK
            o_g, lse_g = _dil_attn_call(qkvs[g], g, min(DIL_ROWS, S // dil))
            ods.append(o_g)
            lses.append(lse_g)

        wf = jnp.concatenate([wl[:, OFF_MERGE:], wl[:, OFF_MLA_G:OFF_DQ], wl[:, OFF_DIL_G:OFF_MERGE]],
                             axis=1).astype(BF16)
        x = _merge_call(x, y_lru, o_mla, ods, lses, ng, wf, b_merge[l][None, :],
                        w_lru_o[l].astype(BF16), w_mla_o[l].astype(BF16), w_dil_o[l].astype(BF16),
                        w_out[l].astype(BF16), expand, tm)
    return x
```
